```python
import jax
import jax.numpy as jnp
from jax import lax
import numpy as np

D_MODEL = 1024
BATCH = 2
SEQ = 8192
DEPTH = 2

MIX = D_MODEL // 2
MOBA_HEAD_DIM = 64
MOBA_HEADS = MIX // MOBA_HEAD_DIM
MOBA_BLOCK = 256
MOBA_TOPK = 3
Q_BLOCK = 128
ROPE_THETA = 500000.0
ROPE_DIMS = MOBA_HEAD_DIM // 4
GLA_HEADS = 4
GLA_DK = MIX // 2
GLA_DV = MIX
GLA_GATE_RANK = 16
GLA_TAU = 16.0
GLA_CHUNK = 64
MLSTM_HEADS = 4
MLSTM_DQK = MIX
MLSTM_DV = MIX
MLSTM_CONV = 4
MLSTM_CHUNK = 64
N_BRANCHES = 3
D_FF = ((8 * D_MODEL // 3 + 255) // 256) * 256
N_EXPERTS = 8
TOP_K = 2
MOE_BLOCK = 256
EPS = 1e-6

IN_LAYOUT = (
    ('moba_q', MIX), ('moba_k', MIX), ('moba_v', MIX),
    ('gla_q', GLA_DK), ('gla_k', GLA_DK), ('gla_v', GLA_DV),
    ('gla_a', GLA_GATE_RANK), ('gla_r', GLA_DV),
    ('mlstm_q', MLSTM_DQK), ('mlstm_k', MLSTM_DQK), ('mlstm_v', MLSTM_DV),
    ('mlstm_i', MLSTM_HEADS), ('mlstm_f', MLSTM_HEADS), ('mlstm_o', MLSTM_DV),
    ('merge_gate', N_BRANCHES * D_MODEL),
)
N_IN = sum(size for _, size in IN_LAYOUT)

kernel_name = 'hybrid_moba_gla_mlstm_moe_block'


def _col_offset(name):
    off = 0
    for n, size in IN_LAYOUT:
        if n == name:
            return off
        off += size
    raise KeyError(name)


def _split_in(z):
    out = {}
    off = 0
    for n, size in IN_LAYOUT:
        out[n] = z[..., off:off + size]
        off += size
    return out


def rmsnorm(x, g):
    xf = x.astype(jnp.float32)
    y = xf * lax.rsqrt(jnp.mean(xf * xf, axis=-1, keepdims=True) + EPS)
    return (y * g.astype(jnp.float32)).astype(x.dtype)


def to_heads(t, nh):
    b, s, _ = t.shape
    return t.reshape(b, s, nh, -1).transpose(0, 2, 1, 3)


def from_heads(t):
    b, nh, s, hd = t.shape
    return t.transpose(0, 2, 1, 3).reshape(b, s, nh * hd)


def partial_rope(x, pos):
    half = ROPE_DIMS // 2
    inv_freq = jnp.power(ROPE_THETA, -jnp.arange(half, dtype=jnp.float32) * 2.0 / ROPE_DIMS)
    ang = pos.astype(jnp.float32)[:, None] * inv_freq[None, :]
    cos, sin = jnp.cos(ang), jnp.sin(ang)
    xr = x[..., :ROPE_DIMS].astype(jnp.float32)
    x1, x2 = xr[..., :half], xr[..., half:]
    rot = jnp.concatenate([x1 * cos - x2 * sin, x2 * cos + x1 * sin], axis=-1)
    return jnp.concatenate([rot.astype(x.dtype), x[..., ROPE_DIMS:]], axis=-1)


def moba_attention(q, k, v):
    f32 = jnp.float32
    b, nh, s, hd = q.shape
    nb = -(-s // MOBA_BLOCK)
    n_sel = min(MOBA_TOPK, nb)
    pad = nb * MOBA_BLOCK - s
    kb = jnp.pad(k, ((0, 0), (0, 0), (0, pad), (0, 0))).reshape(b, nh, nb, MOBA_BLOCK, hd)
    vb = jnp.pad(v, ((0, 0), (0, 0), (0, pad), (0, 0))).reshape(b, nh, nb, MOBA_BLOCK, hd)
    k_mean = jnp.mean(kb.astype(f32), axis=3)
    nq = s // Q_BLOCK
    q_blocks = q.reshape(b, nh, nq, Q_BLOCK, hd).transpose(2, 0, 1, 3, 4)
    bi = jnp.arange(b)[:, None, None, None]
    hi = jnp.arange(nh)[None, :, None, None]
    blk_ids = jnp.arange(nb)
    scale = hd ** -0.5
    n_cols = n_sel * MOBA_BLOCK

    def one_block(args):
        qb, c = args
        q_pos = c * Q_BLOCK + jnp.arange(Q_BLOCK)
        cur = (c * Q_BLOCK) // MOBA_BLOCK
        gate = jnp.einsum('bhqd,bhnd->bhqn', qb.astype(f32), k_mean)
        gate = jnp.where(blk_ids < cur, gate, -jnp.inf)
        _, sel = lax.top_k(gate, n_sel)
        sel_ok = jnp.arange(n_sel) < cur
        k_sel = kb[bi, hi, sel]
        v_sel = vb[bi, hi, sel]
        s_sel = jnp.einsum('bhqd,bhqrtd->bhqrt', qb, k_sel).astype(f32) * scale
        s_sel = jnp.where(sel_ok[:, None], s_sel, -jnp.inf)
        k_own = lax.dynamic_index_in_dim(kb, cur, axis=2, keepdims=False)
        v_own = lax.dynamic_index_in_dim(vb, cur, axis=2, keepdims=False)
        s_own = jnp.einsum('bhqd,bhtd->bhqt', qb, k_own).astype(f32) * scale
        own_pos = cur * MOBA_BLOCK + jnp.arange(MOBA_BLOCK)
        s_own = jnp.where(own_pos[None, :] <= q_pos[:, None], s_own, -jnp.inf)
        p = jax.nn.softmax(jnp.concatenate([s_sel.reshape(b, nh, Q_BLOCK, n_cols), s_own], axis=-1), axis=-1)
        p_sel = p[..., :n_cols].reshape(b, nh, Q_BLOCK, n_sel, MOBA_BLOCK).astype(v.dtype)
        p_own = p[..., n_cols:].astype(v.dtype)
        return (jnp.einsum('bhqrt,bhqrtd->bhqd', p_sel, v_sel)
                + jnp.einsum('bhqt,bhtd->bhqd', p_own, v_own))

    out = lax.map(one_block, (q_blocks, jnp.arange(nq)))
    return out.transpose(1, 2, 0, 3, 4).reshape(b, nh, s, hd)


def gla_chunked(q, k, v, log_a):
    b, nh, s, dk = q.shape
    dv = v.shape[-1]
    L = GLA_CHUNK
    nc = s // L
    q = q * dk ** -0.5
    qc = q.reshape(b, nh, nc, L, dk)
    kc = k.reshape(b, nh, nc, L, dk)
    vc = v.reshape(b, nh, nc, L, dv)
    cum = jnp.cumsum(log_a.reshape(b, nh, nc, L, dk), axis=3)
    cum_last = cum[:, :, :, -1:, :]
    q_dec = qc * jnp.exp(cum)
    k_inv = kc * jnp.exp(-cum)
    k_to_end = kc * jnp.exp(cum_last - cum)
    causal = jnp.tril(jnp.ones((L, L), dtype=bool))
    att = jnp.where(causal, jnp.einsum('bhcid,bhcjd->bhcij', q_dec, k_inv), 0.0)
    o_intra = jnp.einsum('bhcij,bhcje->bhcie', att, vc)

    def step(state, inp):
        dec, kk, vv = inp
        new = dec[..., None] * state + jnp.einsum('bhjd,bhje->bhde', kk, vv)
        return new, state

    xs = (jnp.exp(cum_last[:, :, :, 0, :]).transpose(2, 0, 1, 3),
          k_to_end.transpose(2, 0, 1, 3, 4), vc.transpose(2, 0, 1, 3, 4))
    _, s_prev = lax.scan(step, jnp.zeros((b, nh, dk, dv), jnp.float32), xs)
    o_inter = jnp.einsum('bhcid,cbhde->bhcie', q_dec, s_prev)
    return (o_intra + o_inter).reshape(b, nh, s, dv)


def mlstm_chunked(q, k, v, i_pre, f_pre):
    b, nh, s, dk = q.shape
    dv = v.shape[-1]
    L = MLSTM_CHUNK
    nc = s // L
    k = k * dk ** -0.5
    qc = q.reshape(b, nh, nc, L, dk)
    kc = k.reshape(b, nh, nc, L, dk)
    vc = v.reshape(b, nh, nc, L, dv)
    ic = i_pre.reshape(b, nh, nc, L)
    cum = jnp.cumsum(jax.nn.log_sigmoid(f_pre).reshape(b, nh, nc, L), axis=-1)
    cum_last = cum[..., -1]
    a_end = cum_last[..., None] - cum + ic

    def step(carry, inp):
        c_st, n_st, m_st = carry
        bl, aa, kk, vv = inp
        m_new = jnp.maximum(bl + m_st, jnp.max(aa, axis=-1))
        w = jnp.exp(aa - m_new[..., None])
        dec = jnp.exp(bl + m_st - m_new)
        c_new = dec[..., None, None] * c_st + jnp.einsum('bhj,bhjd,bhje->bhde', w, kk, vv)
        n_new = dec[..., None] * n_st + jnp.einsum('bhj,bhjd->bhd', w, kk)
        return (c_new, n_new, m_new), (c_st, n_st, m_st)

    init = (jnp.zeros((b, nh, dk, dv), jnp.float32), jnp.zeros((b, nh, dk), jnp.float32),
            jnp.zeros((b, nh), jnp.float32))
    xs = (cum_last.transpose(2, 0, 1), a_end.transpose(2, 0, 1, 3),
          kc.transpose(2, 0, 1, 3, 4), vc.transpose(2, 0, 1, 3, 4))
    _, (c_prev, n_prev, m_prev) = lax.scan(step, init, xs)
    c_prev = c_prev.transpose(1, 2, 0, 3, 4)
    n_prev = n_prev.transpose(1, 2, 0, 3)
    m_prev = m_prev.transpose(1, 2, 0)
    causal = jnp.tril(jnp.ones((L, L), dtype=bool))
    log_d = jnp.where(causal, cum[..., :, None] - cum[..., None, :] + ic[..., None, :], -jnp.inf)
    log_inter = cum + m_prev[..., None]
    m_i = jnp.maximum(log_inter, jnp.max(log_d, axis=-1))
    w_intra = jnp.einsum('bhcid,bhcjd->bhcij', qc, kc) * jnp.exp(log_d - m_i[..., None])
    w_inter = jnp.exp(log_inter - m_i)
    num = (w_inter[..., None] * jnp.einsum('bhcid,bhcde->bhcie', qc, c_prev)
           + jnp.einsum('bhcij,bhcje->bhcie', w_intra, vc))
    den = w_inter * jnp.einsum('bhcid,bhcd->bhci', qc, n_prev) + jnp.sum(w_intra, axis=-1)
    h = num / jnp.maximum(jnp.abs(den), jnp.exp(-m_i))[..., None]
    return h.reshape(b, nh, s, dv)


def causal_depthwise_conv(x, w, bias):
    y = lax.conv_general_dilated(x, w[:, None, :], window_strides=(1,),
                                 padding=[(w.shape[0] - 1, 0)],
                                 dimension_numbers=('NWC', 'WIO', 'NWC'),
                                 feature_group_count=x.shape[-1])
    return y + bias


def hybrid_mixer(h, w_in, b_in, mlstm_conv_w, mlstm_conv_b, gla_gate_w, gla_gate_b,
                 gla_norm_g, w_up_moba, w_up_gla, w_up_mlstm, w_o):
    f32 = jnp.float32
    dt = h.dtype
    b, s, _ = h.shape
    z = _split_in(h @ w_in + b_in)
    pos = jnp.arange(s)

    q = partial_rope(to_heads(z['moba_q'], MOBA_HEADS), pos)
    k = partial_rope(to_heads(z['moba_k'], MOBA_HEADS), pos)
    y_moba = from_heads(moba_attention(q, k, to_heads(z['moba_v'], MOBA_HEADS)))

    log_a = jax.nn.log_sigmoid((z['gla_a'] @ gla_gate_w + gla_gate_b).astype(f32)) / GLA_TAU
    o = gla_chunked(to_heads(z['gla_q'], GLA_HEADS).astype(f32), to_heads(z['gla_k'], GLA_HEADS).astype(f32),
                    to_heads(z['gla_v'], GLA_HEADS).astype(f32), to_heads(log_a, GLA_HEADS))
    o = o * lax.rsqrt(jnp.mean(o * o, axis=-1, keepdims=True) + EPS)
    y_gla = (from_heads(o) * gla_norm_g.astype(f32) * jax.nn.silu(z['gla_r'].astype(f32))).astype(dt)

    qk = jax.nn.silu(causal_depthwise_conv(jnp.concatenate([z['mlstm_q'], z['mlstm_k']], axis=-1),
                                           mlstm_conv_w, mlstm_conv_b))
    mq, mk = qk[..., :MLSTM_DQK], qk[..., MLSTM_DQK:]
    hm = mlstm_chunked(to_heads(mq, MLSTM_HEADS).astype(f32), to_heads(mk, MLSTM_HEADS).astype(f32),
                       to_heads(z['mlstm_v'], MLSTM_HEADS).astype(f32),
                       z['mlstm_i'].astype(f32).transpose(0, 2, 1), z['mlstm_f'].astype(f32).transpose(0, 2, 1))
    y_mlstm = (from_heads(hm) * jax.nn.sigmoid(z['mlstm_o'].astype(f32))).astype(dt)

    g = jax.nn.sigmoid(z['merge_gate'].astype(f32)).astype(dt).reshape(b, s, N_BRANCHES, D_MODEL)
    merged = (g[:, :, 0] * (y_moba @ w_up_moba) + g[:, :, 1] * (y_gla @ w_up_gla)
              + g[:, :, 2] * (y_mlstm @ w_up_mlstm))
    return merged @ w_o


def swiglu(x, w_gate, w_up, w_down):
    return (jax.nn.silu(x @ w_gate) * (x @ w_up)) @ w_down


def moe_swiglu(x, router_w, router_b, w_gate, w_up, w_down):
    b, s, d = x.shape
    t = b * s
    xt = x.reshape(t, d)
    logits = (xt @ router_w).astype(jnp.float32) + router_b.astype(jnp.float32)
    top_logit, top_e = lax.top_k(logits, TOP_K)
    gate = jax.nn.softmax(top_logit, axis=-1)
    n_pairs = t * TOP_K
    e_flat = top_e.reshape(n_pairs)
    tok_flat = jnp.repeat(jnp.arange(t, dtype=jnp.int32), TOP_K)
    g_flat = gate.reshape(n_pairs)
    order = jnp.argsort(e_flat)
    e_s, tok_s, g_s = e_flat[order], tok_flat[order], g_flat[order]
    counts = jnp.bincount(e_flat, length=N_EXPERTS)
    starts = jnp.cumsum(counts) - counts
    padded = (counts + MOE_BLOCK - 1) // MOE_BLOCK * MOE_BLOCK
    pad_ends = jnp.cumsum(padded)
    pad_starts = pad_ends - padded
    dest = pad_starts[e_s] + jnp.arange(n_pairs) - starts[e_s]
    n_rows = ((n_pairs + MOE_BLOCK - 1) // MOE_BLOCK + N_EXPERTS) * MOE_BLOCK
    n_blk = n_rows // MOE_BLOCK
    rows_tok = jnp.full((n_rows,), t, jnp.int32).at[dest].set(tok_s)
    rows_g = jnp.zeros((n_rows,), x.dtype).at[dest].set(g_s.astype(x.dtype))
    blk_e = jnp.minimum(jnp.searchsorted(pad_ends, jnp.arange(n_blk) * MOE_BLOCK, side='right'),
                        N_EXPERTS - 1)
    x_pad = jnp.concatenate([xt, jnp.zeros((1, d), xt.dtype)], axis=0)

    def block(args):
        tok, e = args
        xb = x_pad[tok]
        return swiglu(xb, w_gate[e], w_up[e], w_down[e])

    y_rows = lax.map(block, (rows_tok.reshape(n_blk, MOE_BLOCK), blk_e)).reshape(n_rows, d)
    y_rows = y_rows * rows_g[:, None]
    out = jnp.zeros((t + 1, d), x.dtype).at[rows_tok].add(y_rows)[:t]
    return out.reshape(b, s, d)


def setup_inputs(seed: int = 0) -> dict:
    key = jax.random.key(seed)
    ks = jax.random.split(key, 26)
    f32 = jnp.float32

    def nrm(k, shape, scale):
        return jax.random.normal(k, shape, f32) * scale

    n_dense = (DEPTH + 1) // 2
    n_moe = DEPTH // 2
    f_off = _col_offset('mlstm_f')
    b_in = nrm(ks[3], (DEPTH, N_IN), 0.02)
    b_in = b_in.at[:, f_off:f_off + MLSTM_HEADS].add(jnp.linspace(3.0, 6.0, MLSTM_HEADS))
    return {
        'x': nrm(ks[0], (BATCH, SEQ, D_MODEL), 1.0),
        'attn_norm_g': 1.0 + nrm(ks[1], (DEPTH, D_MODEL), 0.02),
        'w_in': nrm(ks[2], (DEPTH, D_MODEL, N_IN), D_MODEL ** -0.5),
        'b_in': b_in,
        'mlstm_conv_w': nrm(ks[4], (DEPTH, MLSTM_CONV, 2 * MLSTM_DQK), MLSTM_CONV ** -0.5),
        'mlstm_conv_b': nrm(ks[5], (DEPTH, 2 * MLSTM_DQK), 0.02),
        'gla_gate_w': nrm(ks[6], (DEPTH, GLA_GATE_RANK, GLA_DK), GLA_GATE_RANK ** -0.5),
        'gla_gate_b': nrm(ks[7], (DEPTH, GLA_DK), 0.02),
        'gla_norm_g': 1.0 + nrm(ks[8], (DEPTH, GLA_DV), 0.02),
        'w_up_moba': nrm(ks[9], (DEPTH, MIX, D_MODEL), MIX ** -0.5),
        'w_up_gla': nrm(ks[10], (DEPTH, MIX, D_MODEL), MIX ** -0.5),
        'w_up_mlstm': nrm(ks[11], (DEPTH, MIX, D_MODEL), MIX ** -0.5),
        'w_o': nrm(ks[12], (DEPTH, D_MODEL, D_MODEL), D_MODEL ** -0.5),
        'ffn_norm_g': 1.0 + nrm(ks[13], (DEPTH, D_MODEL), 0.02),
        'ffn_w_gate': nrm(ks[14], (n_dense, D_MODEL, D_FF), D_MODEL ** -0.5),
        'ffn_w_up': nrm(ks[15], (n_dense, D_MODEL, D_FF), D_MODEL ** -0.5),
        'ffn_w_down': nrm(ks[16], (n_dense, D_FF, D_MODEL), D_FF ** -0.5),
        'router_w': nrm(ks[17], (n_moe, D_MODEL, N_EXPERTS), D_MODEL ** -0.5),
        'router_b': nrm(ks[18], (n_moe, N_EXPERTS), 0.01),
        'moe_w_gate': nrm(ks[19], (n_moe, N_EXPERTS, D_MODEL, D_FF), D_MODEL ** -0.5),
        'moe_w_up': nrm(ks[20], (n_moe, N_EXPERTS, D_MODEL, D_FF), D_MODEL ** -0.5),
        'moe_w_down': nrm(ks[21], (n_moe, N_EXPERTS, D_FF, D_MODEL), D_FF ** -0.5),
        'final_norm_g': 1.0 + nrm(ks[22], (D_MODEL,), 0.02),
    }


def reference(x, attn_norm_g, w_in, b_in, mlstm_conv_w, mlstm_conv_b, gla_gate_w, gla_gate_b,
              gla_norm_g, w_up_moba, w_up_gla, w_up_mlstm, w_o, ffn_norm_g, ffn_w_gate, ffn_w_up,
              ffn_w_down, router_w, router_b, moe_w_gate, moe_w_up, moe_w_down, final_norm_g):
    for l in range(DEPTH):
        h = rmsnorm(x, attn_norm_g[l])
        x = x + hybrid_mixer(h, w_in[l], b_in[l], mlstm_conv_w[l], mlstm_conv_b[l], gla_gate_w[l],
                             gla_gate_b[l], gla_norm_g[l], w_up_moba[l], w_up_gla[l], w_up_mlstm[l], w_o[l])
        h = rmsnorm(x, ffn_norm_g[l])
        j = l // 2
        if l % 2 == 0:
            x = x + swiglu(h, ffn_w_gate[j], ffn_w_up[j], ffn_w_down[j])
        else:
            x = x + moe_swiglu(h, router_w[j], router_b[j], moe_w_gate[j], moe_w_up[j], moe_w_down[j])
    return rmsnorm(x, final_norm_g)
```

```python
import functools

import jax
import jax.numpy as jnp
from jax import lax
from jax.experimental import pallas as pl
from jax.experimental.pallas import tpu as pltpu

F32 = jnp.float32
BF16 = jnp.bfloat16
I32 = jnp.int32

D_MODEL = 1024
MIX = D_MODEL // 2
MOBA_HEAD_DIM = 64
MOBA_BLOCK = 256
MOBA_TOPK = 3
ROPE_THETA = 500000.0
ROPE_DIMS = MOBA_HEAD_DIM // 4
GLA_HEADS = 4
GLA_DK = MIX // 2
GLA_DV = MIX
GLA_GATE_RANK = 16
GLA_TAU = 16.0
MLSTM_HEADS = 4
MLSTM_DQK = MIX
MLSTM_DV = MIX
MLSTM_CONV = 4
CHUNK = 64
N_BRANCHES = 3
N_EXPERTS = 8
TOP_K = 2
EPS = 1e-6

LANES = 128
VMEM_LIMIT = 56 * 1024 * 1024

NN = (((1,), (0,)), ((), ()))
NT = (((1,), (1,)), ((), ()))
TN = (((0,), (0,)), ((), ()))


def _dot(a, b, dims=NN):
    return lax.dot_general(a, b, dims, preferred_element_type=F32)


def _dot_f32(a, b, dims=NN):
    return lax.dot_general(a, b, dims, preferred_element_type=F32,
                           precision=lax.Precision.HIGHEST)


def _rms(x, g):
    return x * lax.rsqrt(jnp.mean(x * x, axis=-1, keepdims=True) + EPS) * g


def _sigmoid(x):
    return 1.0 / (1.0 + jnp.exp(-x))


def _log_sigmoid(x):
    return jnp.minimum(x, 0.0) - jnp.log(1.0 + jnp.exp(-jnp.abs(x)))


def _params(*sem):
    return pltpu.CompilerParams(dimension_semantics=sem, vmem_limit_bytes=VMEM_LIMIT)


def _resident(shape):
    n = len(shape)
    return pl.BlockSpec(shape, lambda *_: (0,) * n, pipeline_mode=pl.Buffered(1))


def _norm_mm_kernel(x_ref, g_ref, w_ref, b_ref, o_ref):
    h = _rms(x_ref[...], g_ref[...]).astype(BF16)
    o_ref[...] = (_dot(h, w_ref[...]) + b_ref[...]).astype(o_ref.dtype)


def norm_matmul(x2d, g, w, b, *, tm, out_dtype=F32):
    t, d = x2d.shape
    n = w.shape[1]
    return pl.pallas_call(
        _norm_mm_kernel,
        grid=(t // tm,),
        in_specs=[pl.BlockSpec((tm, d), lambda i: (i, 0)), _resident((1, d)),
                  _resident((d, n)), _resident((1, n))],
        out_specs=pl.BlockSpec((tm, n), lambda i: (i, 0)),
        out_shape=jax.ShapeDtypeStruct((t, n), out_dtype),
        compiler_params=_params("parallel"),
        name="norm_matmul",
    )(x2d, g.reshape(1, d), w, b.reshape(1, n))


def _rope_tables(s):
    half = ROPE_DIMS // 2
    inv_freq = jnp.power(ROPE_THETA, -jnp.arange(half, dtype=F32) * 2.0 / ROPE_DIMS)
    ang = jnp.arange(s).astype(F32)[:, None] * inv_freq[None, :]
    cos, sin = jnp.cos(ang), jnp.sin(ang)
    ones = jnp.ones((s, MOBA_HEAD_DIM - ROPE_DIMS), F32)
    zeros = jnp.zeros((s, MOBA_HEAD_DIM - ROPE_DIMS), F32)
    zh = jnp.zeros((s, half), F32)
    c = jnp.concatenate([cos, cos, ones], axis=1)
    sa = jnp.concatenate([zh, sin, zeros], axis=1)
    sb = jnp.concatenate([-sin, zh, zeros], axis=1)
    rep = LANES // MOBA_HEAD_DIM
    return tuple(jnp.tile(a, (1, rep)) for a in (c, sa, sb))


def _moba_proj_kernel(x_ref, g_ref, w_ref, b_ref, c_ref, sa_ref, sb_ref,
                      q_ref, k_ref, v_ref, km_ref):
    h = _rms(x_ref[...], g_ref[...]).astype(BF16)
    z = _dot(h, w_ref[...]) + b_ref[...]
    tm = z.shape[0]
    rep = MIX // LANES
    c = jnp.concatenate([c_ref[...]] * rep, axis=1)
    sa = jnp.concatenate([sa_ref[...]] * rep, axis=1)
    sb = jnp.concatenate([sb_ref[...]] * rep, axis=1)
    half = ROPE_DIMS // 2

    def rot(t):
        return t * c + pltpu.roll(t, half, 1) * sa + pltpu.roll(t, MIX - half, 1) * sb

    q = rot(z[:, :MIX]) * (MOBA_HEAD_DIM ** -0.5)
    k = rot(z[:, MIX:2 * MIX])
    q_ref[...] = q.astype(BF16)
    k_ref[...] = k.astype(BF16)
    v_ref[...] = z[:, 2 * MIX:].astype(BF16)
    nb = tm // MOBA_BLOCK
    km_ref[...] = jnp.mean(k.reshape(nb, MOBA_BLOCK, MIX), axis=1).reshape(nb, 1, MIX)


def moba_proj(x2d, g, w, b, tables, s, *, tm):
    t, d = x2d.shape
    n = w.shape[1]
    c, sa, sb = tables
    spt = s // tm
    tab = pl.BlockSpec((tm, LANES), lambda i: (i % spt, 0))
    row = pl.BlockSpec((tm, MIX), lambda i: (i, 0))
    nb = tm // MOBA_BLOCK
    return pl.pallas_call(
        _moba_proj_kernel,
        grid=(t // tm,),
        in_specs=[pl.BlockSpec((tm, d), lambda i: (i, 0)), _resident((1, d)),
                  _resident((d, n)), _resident((1, n)), tab, tab, tab],
        out_specs=[row, row, row, pl.BlockSpec((nb, 1, MIX), lambda i: (i, 0, 0))],
        out_shape=[jax.ShapeDtypeStruct((t, MIX), BF16)] * 3
        + [jax.ShapeDtypeStruct((t // MOBA_BLOCK, 1, MIX), F32)],
        compiler_params=_params("parallel"),
        name="moba_proj",
    )(x2d, g.reshape(1, d), w, b.reshape(1, n), c, sa, sb)


MOBA_TQ = 128


def _moba_attn_kernel(q_ref, k_ref, v_ref, km_ref, o_ref):
    qi = pl.program_id(2)
    cur = (qi * MOBA_TQ) // MOBA_BLOCK
    nb = km_ref.shape[1]
    hd = MOBA_HEAD_DIM
    neg = -jnp.inf
    outs = []
    for hh in range(LANES // hd):
        hs = slice(hh * hd, (hh + 1) * hd)
        q = q_ref[0, :, hs]
        km = km_ref[0, :, hs]
        km_hi = km.astype(BF16)
        km_lo = (km - km_hi.astype(F32)).astype(BF16)
        gate = _dot(q, km_hi, NT) + _dot(q, km_lo, NT)
        blk = lax.broadcasted_iota(I32, gate.shape, 1)
        gate = jnp.where(blk < cur, gate, neg)
        sel = jnp.zeros(gate.shape, F32)
        for _ in range(MOBA_TOPK):
            m = jnp.max(gate, axis=1, keepdims=True)
            first = jnp.min(jnp.where(gate == m, blk, nb), axis=1, keepdims=True)
            first = jnp.where(m > neg, first, nb)
            pick = blk == first
            sel = jnp.where(pick, 1.0, sel)
            gate = jnp.where(pick, neg, gate)

        own = pl.multiple_of(cur * MOBA_BLOCK, MOBA_BLOCK)
        k_own = k_ref[0, pl.ds(own, MOBA_BLOCK), hs]
        v_own = v_ref[0, pl.ds(own, MOBA_BLOCK), hs]
        s = _dot(q, k_own, NT)
        q_pos = qi * MOBA_TQ + lax.broadcasted_iota(I32, s.shape, 0)
        k_pos = own + lax.broadcasted_iota(I32, s.shape, 1)
        s = jnp.where(k_pos <= q_pos, s, neg)
        m0 = jnp.max(s, axis=1, keepdims=True)
        p = jnp.exp(s - m0)
        l0 = jnp.sum(p, axis=1, keepdims=True)
        acc0 = _dot(p.astype(BF16), v_own)

        def body(j, carry):
            m_i, l_i, acc = carry
            off = pl.multiple_of(j * MOBA_BLOCK, MOBA_BLOCK)
            kj = k_ref[0, pl.ds(off, MOBA_BLOCK), hs]
            vj = v_ref[0, pl.ds(off, MOBA_BLOCK), hs]
            sj = _dot(q, kj, NT)
            selj = jnp.sum(jnp.where(blk == j, sel, 0.0), axis=1, keepdims=True) > 0.0
            sj = jnp.where(selj, sj, neg)
            m_new = jnp.maximum(m_i, jnp.max(sj, axis=1, keepdims=True))
            alpha = jnp.exp(m_i - m_new)
            pj = jnp.exp(sj - m_new)
            l_new = alpha * l_i + jnp.sum(pj, axis=1, keepdims=True)
            acc_new = alpha * acc + _dot(pj.astype(BF16), vj)
            return m_new, l_new, acc_new

        _, l_f, acc_f = lax.fori_loop(0, cur, body, (m0, l0, acc0))
        outs.append(acc_f / l_f)
    o_ref[0] = jnp.concatenate(outs, axis=1).astype(o_ref.dtype)


def moba_attention(q, k, v, km):
    b, s, _ = q.shape
    nb = km.shape[1]
    qspec = pl.BlockSpec((1, MOBA_TQ, LANES), lambda bi, hp, qi: (bi, qi, hp))
    kvspec = pl.BlockSpec((1, s, LANES), lambda bi, hp, qi: (bi, 0, hp))
    return pl.pallas_call(
        _moba_attn_kernel,
        grid=(b, MIX // LANES, s // MOBA_TQ),
        in_specs=[qspec, kvspec, kvspec,
                  pl.BlockSpec((1, nb, LANES), lambda bi, hp, qi: (bi, 0, hp))],
        out_specs=qspec,
        out_shape=jax.ShapeDtypeStruct((b, s, MIX), BF16),
        compiler_params=_params("parallel", "parallel", "arbitrary"),
        name="moba_attention",
    )(q, k, v, km)


SCAN_ROWS = 512


def _gla_kernel(qk_ref, v_ref, r_ref, a_ref, gw_ref, gb_ref, ng_ref, o_ref, st_ref):
    @pl.when(pl.program_id(1) == 0)
    def _():
        st_ref[...] = jnp.zeros_like(st_ref)

    dk = GLA_DK // GLA_HEADS
    dv = GLA_DV // GLA_HEADS
    L = CHUNK
    ri = lax.broadcasted_iota(I32, (L, L), 0)
    ci = lax.broadcasted_iota(I32, (L, L), 1)
    causal = ci <= ri
    tri = jnp.where(causal, 1.0, 0.0).astype(F32)
    gw = gw_ref[...]
    gb = gb_ref[...]
    ng = ng_ref[...]

    def chunk(c, _):
        rows = pl.ds(pl.multiple_of(c * L, L), L)
        pre = _dot_f32(a_ref[rows, :], gw) + gb
        log_a = _log_sigmoid(pre) / GLA_TAU
        cum = _dot_f32(tri, log_a)
        cum_last = cum[L - 1:L, :]
        qk = qk_ref[rows, :]
        q = qk[:, :GLA_DK] * (dk ** -0.5)
        k = qk[:, GLA_DK:]
        q_dec = (q * jnp.exp(cum)).astype(BF16)
        k_inv = (k * jnp.exp(-cum)).astype(BF16)
        k_end = (k * jnp.exp(cum_last - cum)).astype(BF16)
        dec = jnp.exp(cum_last)
        v = v_ref[rows, :].astype(BF16)
        r = r_ref[rows, :]
        outs = []
        for h in range(GLA_HEADS):
            hs = slice(h * dk, (h + 1) * dk)
            vs = slice(h * dv, (h + 1) * dv)
            st = st_ref[h]
            att = jnp.where(causal, _dot(q_dec[:, hs], k_inv[:, hs], NT), 0.0)
            o = _dot(att.astype(BF16), v[:, vs]) + _dot(q_dec[:, hs], st.astype(BF16), NT)
            st_ref[h] = st * dec[:, hs] + _dot(v[:, vs], k_end[:, hs], TN)
            o = o * lax.rsqrt(jnp.mean(o * o, axis=-1, keepdims=True) + EPS)
            outs.append(o)
        o_all = jnp.concatenate(outs, axis=1)
        o_ref[rows, :] = (o_all * ng * (r * _sigmoid(r))).astype(o_ref.dtype)
        return 0

    lax.fori_loop(0, SCAN_ROWS // L, chunk, 0)


def gla_scan(z, gate_w_pad, gate_b, norm_g, b, s):
    t = z.shape[0]
    nc = s // SCAN_ROWS
    wide = lambda col: pl.BlockSpec((SCAN_ROWS, MIX), lambda bi, c: (bi * nc + c, col))
    dk = GLA_DK // GLA_HEADS
    dv = GLA_DV // GLA_HEADS
    return pl.pallas_call(
        _gla_kernel,
        grid=(b, nc),
        in_specs=[wide(0), wide(1), wide(2),
                  pl.BlockSpec((SCAN_ROWS, LANES), lambda bi, c: (bi * nc + c, 3 * MIX // LANES)),
                  _resident((LANES, GLA_DK)), _resident((1, GLA_DK)), _resident((1, GLA_DV))],
        out_specs=pl.BlockSpec((SCAN_ROWS, MIX), lambda bi, c: (bi * nc + c, 0)),
        out_shape=jax.ShapeDtypeStruct((t, MIX), BF16),
        scratch_shapes=[pltpu.VMEM((GLA_HEADS, dv, dk), F32)],
        compiler_params=_params("parallel", "arbitrary"),
        name="gla_scan",
    )(z, z, z, z, gate_w_pad, gate_b.reshape(1, GLA_DK), norm_g.reshape(1, GLA_DV))


TAIL = 8


def _mlstm_kernel(qk_ref, v_ref, og_ref, ifc_ref, ifr_ref, cw_ref, cb_ref, o_ref,
                  tail_ref, ct_ref, n_ref, m_ref, qkc_ref):
    @pl.when(pl.program_id(1) == 0)
    def _():
        tail_ref[...] = jnp.zeros_like(tail_ref)
        ct_ref[...] = jnp.zeros_like(ct_ref)
        n_ref[...] = jnp.zeros_like(n_ref)
        m_ref[...] = jnp.zeros_like(m_ref)

    nh = MLSTM_HEADS
    dk = MLSTM_DQK // nh
    dv = MLSTM_DV // nh
    L = CHUNK
    rows_n = qk_ref.shape[0]

    x = qk_ref[...]
    xx = jnp.concatenate([tail_ref[...], x], axis=0)
    cw = cw_ref[...]
    y = cb_ref[...] + sum(
        cw[w:w + 1, :] * xx[TAIL - (MLSTM_CONV - 1) + w:TAIL - (MLSTM_CONV - 1) + w + rows_n, :]
        for w in range(MLSTM_CONV))
    tail_ref[...] = x[rows_n - TAIL:, :]
    qkc_ref[...] = y * _sigmoid(y)

    ri = lax.broadcasted_iota(I32, (L, L), 0)
    ci = lax.broadcasted_iota(I32, (L, L), 1)
    causal = ci <= ri
    tri = jnp.where(causal, 1.0, 0.0).astype(F32)
    tri_t = jnp.where(ri <= ci, 1.0, 0.0).astype(F32)
    neg = -jnp.inf

    def chunk(c, _):
        rows = pl.ds(pl.multiple_of(c * L, L), L)
        qkc = qkc_ref[rows, :]
        q = qkc[:, :MLSTM_DQK].astype(BF16)
        kf = qkc[:, MLSTM_DQK:] * (dk ** -0.5)
        k = kf.astype(BF16)
        v = v_ref[rows, :].astype(BF16)
        ifc = ifc_ref[rows, :]
        ifr = ifr_ref[c]
        cum_c = _dot_f32(tri, _log_sigmoid(ifc))
        cum_r = _dot_f32(_log_sigmoid(ifr), tri_t)
        outs = []
        for h in range(nh):
            hs = slice(h * dk, (h + 1) * dk)
            vs = slice(h * dv, (h + 1) * dv)
            cum_i = cum_c[:, nh + h:nh + h + 1]
            i_i = ifc[:, h:h + 1]
            cum_j = cum_r[nh + h:nh + h + 1, :]
            i_j = ifr[h:h + 1, :]
            m_prev = m_ref[h:h + 1, 0:1]
            ct = ct_ref[h]
            n_prev = n_ref[h:h + 1, :]
            cum_last = cum_i[L - 1:L, :]

            log_d = jnp.where(causal, cum_i - cum_j + i_j, neg)
            log_inter = cum_i + m_prev
            m_i = jnp.maximum(log_inter, jnp.max(log_d, axis=1, keepdims=True))
            w_intra = _dot(q[:, hs], k[:, hs], NT) * jnp.exp(log_d - m_i)
            w_inter = jnp.exp(log_inter - m_i)
            num = (w_inter * _dot(q[:, hs], ct.astype(BF16), NT)
                   + _dot(w_intra.astype(BF16), v[:, vs]))
            qn = jnp.sum(qkc[:, hs] * n_prev, axis=1, keepdims=True)
            den = w_inter * qn + jnp.sum(w_intra, axis=1, keepdims=True)
            outs.append(num / jnp.maximum(jnp.abs(den), jnp.exp(-m_i)))

            a_end = cum_last - cum_i + i_i
            m_new = jnp.maximum(cum_last + m_prev, jnp.max(a_end, axis=0, keepdims=True))
            w = jnp.exp(a_end - m_new)
            dec = jnp.exp(cum_last + m_prev - m_new)
            wk = w * kf[:, hs]
            ct_ref[h] = dec * ct + _dot(v[:, vs], wk.astype(BF16), TN)
            n_ref[h:h + 1, :] = dec * n_prev + jnp.sum(wk, axis=0, keepdims=True)
            m_ref[h:h + 1, :] = jnp.broadcast_to(m_new, (1, LANES))
        og = og_ref[rows, :]
        o_ref[rows, :] = (jnp.concatenate(outs, axis=1) * _sigmoid(og)).astype(o_ref.dtype)
        return 0

    lax.fori_loop(0, rows_n // L, chunk, 0)


def mlstm_scan(z, ifr, conv_w, conv_b, b, s):
    t = z.shape[0]
    nc = s // SCAN_ROWS
    cpt = SCAN_ROWS // CHUNK
    dk = MLSTM_DQK // MLSTM_HEADS
    dv = MLSTM_DV // MLSTM_HEADS
    row = lambda w, col: pl.BlockSpec((SCAN_ROWS, w), lambda bi, c: (bi * nc + c, col))
    return pl.pallas_call(
        _mlstm_kernel,
        grid=(b, nc),
        in_specs=[row(2 * MLSTM_DQK, 0), row(MIX, 2), row(MIX, 3),
                  row(LANES, 4 * MIX // LANES),
                  pl.BlockSpec((cpt, 2 * MLSTM_HEADS, CHUNK), lambda bi, c: (bi * nc + c, 0, 0)),
                  _resident((MLSTM_CONV, 2 * MLSTM_DQK)), _resident((1, 2 * MLSTM_DQK))],
        out_specs=row(MIX, 0),
        out_shape=jax.ShapeDtypeStruct((t, MIX), BF16),
        scratch_shapes=[pltpu.VMEM((TAIL, 2 * MLSTM_DQK), F32),
                        pltpu.VMEM((MLSTM_HEADS, dv, dk), F32),
                        pltpu.VMEM((8, dk), F32),
                        pltpu.VMEM((8, LANES), F32),
                        pltpu.VMEM((SCAN_ROWS, 2 * MLSTM_DQK), F32)],
        compiler_params=_params("parallel", "arbitrary"),
        name="mlstm_scan",
    )(z, z, z, z, ifr, conv_w, conv_b.reshape(1, -1))


def _merge_kernel(x_ref, g_ref, wg_ref, bg_ref, ya_ref, yb_ref, yc_ref,
                  wa_ref, wb_ref, wc_ref, wo_ref, o_ref):
    x = x_ref[...]
    h = _rms(x, g_ref[...]).astype(BF16)
    d = D_MODEL
    merged = None
    for i, (y_ref, w_ref) in enumerate(((ya_ref, wa_ref), (yb_ref, wb_ref), (yc_ref, wc_ref))):
        gate = _sigmoid(_dot(h, wg_ref[:, i * d:(i + 1) * d]) + bg_ref[:, i * d:(i + 1) * d])
        term = gate * _dot(y_ref[...], w_ref[...])
        merged = term if merged is None else merged + term
    o_ref[...] = x + _dot(merged.astype(BF16), wo_ref[...])


def merge_out(x2d, g, w_gate, b_gate, ya, yb, yc, wa, wb, wc, wo, *, tm):
    t, d = x2d.shape
    row = lambda w: pl.BlockSpec((tm, w), lambda i: (i, 0))
    return pl.pallas_call(
        _merge_kernel,
        grid=(t // tm,),
        in_specs=[row(d), _resident((1, d)), _resident((d, N_BRANCHES * d)),
                  _resident((1, N_BRANCHES * d)), row(MIX), row(MIX), row(MIX),
                  _resident((MIX, d)), _resident((MIX, d)), _resident((MIX, d)),
                  _resident((d, d))],
        out_specs=row(d),
        out_shape=jax.ShapeDtypeStruct((t, d), F32),
        compiler_params=_params("parallel"),
        name="merge_out",
    )(x2d, g.reshape(1, d), w_gate, b_gate.reshape(1, -1), ya, yb, yc, wa, wb, wc, wo)


def _ffn_kernel(x_ref, g_ref, wg_ref, wu_ref, wd_ref, o_ref, h_ref, acc_ref):
    f = pl.program_id(1)

    @pl.when(f == 0)
    def _():
        h_ref[...] = _rms(x_ref[...], g_ref[...]).astype(BF16)
        acc_ref[...] = jnp.zeros_like(acc_ref)

    h = h_ref[...]
    gt = _dot(h, wg_ref[...])
    up = _dot(h, wu_ref[...])
    act = (gt * _sigmoid(gt) * up).astype(BF16)
    acc_ref[...] += _dot(act, wd_ref[...])

    @pl.when(f == pl.num_programs(1) - 1)
    def _():
        o_ref[...] = x_ref[...] + acc_ref[...]


def _ff_tile(ff):
    for cand in (1408, 1024, 768, 512, 256, 128):
        if ff % cand == 0:
            return cand
    raise ValueError(f"unsupported d_ff {ff}")


def dense_ffn(x2d, g, wg, wu, wd, *, tm):
    t, d = x2d.shape
    ff = wg.shape[1]
    tf = _ff_tile(ff)
    return pl.pallas_call(
        _ffn_kernel,
        grid=(t // tm, ff // tf),
        in_specs=[pl.BlockSpec((tm, d), lambda i, f: (i, 0)), _resident((1, d)),
                  pl.BlockSpec((d, tf), lambda i, f: (0, f)),
                  pl.BlockSpec((d, tf), lambda i, f: (0, f)),
                  pl.BlockSpec((tf, d), lambda i, f: (f, 0))],
        out_specs=pl.BlockSpec((tm, d), lambda i, f: (i, 0)),
        out_shape=jax.ShapeDtypeStruct((t, d), F32),
        scratch_shapes=[pltpu.VMEM((tm, d), BF16), pltpu.VMEM((tm, d), F32)],
        compiler_params=_params("parallel", "arbitrary"),
        name="dense_ffn",
    )(x2d, g.reshape(1, d), wg, wu, wd)


ROUTER_TM = 512
MOE_ROWS = 512
META_E, META_G, META_RANK = 0, 2, 4


def _router_kernel(x_ref, g_ref, rw_ref, rb_ref, h_ref, meta_ref, cnt_ref, carry_ref):
    @pl.when(pl.program_id(0) == 0)
    def _():
        carry_ref[...] = jnp.zeros_like(carry_ref)

    h = _rms(x_ref[...], g_ref[...])
    h_ref[...] = h
    tm = h.shape[0]
    lane = lax.broadcasted_iota(I32, (tm, LANES), 1)
    neg = -jnp.inf
    logits = jnp.where(lane < N_EXPERTS, _dot_f32(h, rw_ref[...]) + rb_ref[...], neg)
    m1 = jnp.max(logits, axis=1, keepdims=True)
    e1 = jnp.min(jnp.where(logits == m1, lane, LANES), axis=1, keepdims=True)
    rest = jnp.where(lane == e1, neg, logits)
    m2 = jnp.max(rest, axis=1, keepdims=True)
    e2 = jnp.min(jnp.where(rest == m2, lane, LANES), axis=1, keepdims=True)
    ex = jnp.exp(m2 - m1)
    g1 = 1.0 / (1.0 + ex)
    g2 = ex / (1.0 + ex)
    oh1 = jnp.where(lane == e1, 1.0, 0.0)
    oh2 = jnp.where(lane == e2, 1.0, 0.0)
    cnt = oh1 + oh2
    ri = lax.broadcasted_iota(I32, (tm, tm), 0)
    ci = lax.broadcasted_iota(I32, (tm, tm), 1)
    strict = jnp.where(ci < ri, 1.0, 0.0).astype(BF16)
    excl = _dot(strict, cnt.astype(BF16)) + carry_ref[...]
    r1 = jnp.sum(excl * oh1, axis=1, keepdims=True)
    r2 = jnp.sum(excl * oh2, axis=1, keepdims=True)
    carry_ref[...] += jnp.sum(cnt, axis=0, keepdims=True)
    cnt_ref[...] = carry_ref[...]
    meta = jnp.zeros((tm, LANES), F32)
    for pos, val in ((META_E, e1.astype(F32)), (META_E + 1, e2.astype(F32)),
                     (META_G, g1), (META_G + 1, g2), (META_RANK, r1), (META_RANK + 1, r2)):
        meta = jnp.where(lane == pos, val, meta)
    meta_ref[...] = meta


def moe_router(x2d, g, rw_pad, rb_pad):
    t, d = x2d.shape
    tm = ROUTER_TM
    return pl.pallas_call(
        _router_kernel,
        grid=(t // tm,),
        in_specs=[pl.BlockSpec((tm, d), lambda i: (i, 0)), _resident((1, d)),
                  _resident((d, LANES)), _resident((1, LANES))],
        out_specs=[pl.BlockSpec((tm, d), lambda i: (i, 0)),
                   pl.BlockSpec((tm, LANES), lambda i: (i, 0)),
                   pl.BlockSpec((1, LANES), lambda i: (0, 0))],
        out_shape=[jax.ShapeDtypeStruct((t, d), F32), jax.ShapeDtypeStruct((t, LANES), F32),
                   jax.ShapeDtypeStruct((1, LANES), F32)],
        scratch_shapes=[pltpu.VMEM((1, LANES), F32)],
        compiler_params=_params("arbitrary"),
        name="moe_router",
    )(x2d, g.reshape(1, d), rw_pad, rb_pad)


GATHER_ROWS = 256


def _row_copy(src_hbm, dst_ref, src_row, dst_row, sem):
    return pltpu.make_async_copy(src_hbm.at[pl.ds(src_row, 1)], dst_ref.at[pl.ds(dst_row, 1)], sem)


def _gather_kernel(idx_ref, src_hbm, o_ref, sem):
    n = o_ref.shape[0]

    def start(r, _):
        _row_copy(src_hbm, o_ref, idx_ref[0, 0, r], r, sem).start()
        return 0

    def wait(r, _):
        _row_copy(src_hbm, o_ref, 0, r, sem).wait()
        return 0

    lax.fori_loop(0, n, start, 0)
    lax.fori_loop(0, n, wait, 0)


def gather_rows(src, idx):
    n = idx.shape[0]
    d = src.shape[1]
    nblk = n // GATHER_ROWS
    return pl.pallas_call(
        _gather_kernel,
        grid=(nblk,),
        in_specs=[pl.BlockSpec((1, 1, GATHER_ROWS), lambda i: (i, 0, 0), memory_space=pltpu.SMEM),
                  pl.BlockSpec(memory_space=pl.ANY)],
        out_specs=pl.BlockSpec((GATHER_ROWS, d), lambda i: (i, 0)),
        out_shape=jax.ShapeDtypeStruct((n, d), src.dtype),
        scratch_shapes=[pltpu.SemaphoreType.DMA(())],
        compiler_params=_params("arbitrary"),
        name="gather_rows",
    )(idx.reshape(nblk, 1, GATHER_ROWS), src)


def _grouped_ffn_kernel(be_ref, bv_ref, x_ref, wg_ref, wu_ref, wd_ref, o_ref, acc_ref):
    i = pl.program_id(0)
    f = pl.program_id(1)

    @pl.when(f == 0)
    def _():
        acc_ref[...] = jnp.zeros_like(acc_ref)

    @pl.when(bv_ref[i] > 0)
    def _():
        h = x_ref[...].astype(BF16)
        gt = _dot(h, wg_ref[...])
        up = _dot(h, wu_ref[...])
        act = (gt * _sigmoid(gt) * up).astype(BF16)
        acc_ref[...] += _dot(act, wd_ref[...])

    @pl.when(f == pl.num_programs(1) - 1)
    def _():
        o_ref[...] = acc_ref[...]


def grouped_ffn(xs, blk_e, blk_valid, wg, wu, wd):
    n, d = xs.shape
    ff = wg.shape[2]
    tf = _ff_tile(ff)
    nblk = n // MOE_ROWS
    grid_spec = pltpu.PrefetchScalarGridSpec(
        num_scalar_prefetch=2,
        grid=(nblk, ff // tf),
        in_specs=[pl.BlockSpec((MOE_ROWS, d), lambda i, f, be, bv: (i, 0)),
                  pl.BlockSpec((None, d, tf), lambda i, f, be, bv: (be[i], 0, f)),
                  pl.BlockSpec((None, d, tf), lambda i, f, be, bv: (be[i], 0, f)),
                  pl.BlockSpec((None, tf, d), lambda i, f, be, bv: (be[i], f, 0))],
        out_specs=pl.BlockSpec((MOE_ROWS, d), lambda i, f, be, bv: (i, 0)),
        scratch_shapes=[pltpu.VMEM((MOE_ROWS, d), F32)],
    )
    return pl.pallas_call(
        _grouped_ffn_kernel,
        grid_spec=grid_spec,
        out_shape=jax.ShapeDtypeStruct((n, d), F32),
        compiler_params=_params("parallel", "arbitrary"),
        name="grouped_ffn",
    )(blk_e, blk_valid, xs, wg, wu, wd)


COMBINE_TM = 256


def _combine_kernel(d1_ref, d2_ref, x_ref, meta_ref, fg_ref, ys_hbm, o_ref, b1_ref, b2_ref, sem):
    n = x_ref.shape[0]

    def start(r, _):
        _row_copy(ys_hbm, b1_ref, d1_ref[0, 0, r], r, sem.at[0]).start()
        _row_copy(ys_hbm, b2_ref, d2_ref[0, 0, r], r, sem.at[1]).start()
        return 0

    def wait(r, _):
        _row_copy(ys_hbm, b1_ref, 0, r, sem.at[0]).wait()
        _row_copy(ys_hbm, b2_ref, 0, r, sem.at[1]).wait()
        return 0

    lax.fori_loop(0, n, start, 0)
    lax.fori_loop(0, n, wait, 0)
    meta = meta_ref[...]
    g1 = meta[:, META_G:META_G + 1]
    g2 = meta[:, META_G + 1:META_G + 2]
    y = x_ref[...] + (g1 * b1_ref[...] + g2 * b2_ref[...])
    o_ref[...] = _rms(y, fg_ref[...])


def moe_combine_norm(x2d, meta, dest1, dest2, ys, final_g):
    t, d = x2d.shape
    tm = COMBINE_TM
    nblk = t // tm
    idx = pl.BlockSpec((1, 1, tm), lambda i: (i, 0, 0), memory_space=pltpu.SMEM)
    return pl.pallas_call(
        _combine_kernel,
        grid=(nblk,),
        in_specs=[idx, idx, pl.BlockSpec((tm, d), lambda i: (i, 0)),
                  pl.BlockSpec((tm, LANES), lambda i: (i, 0)), _resident((1, d)),
                  pl.BlockSpec(memory_space=pl.ANY)],
        out_specs=pl.BlockSpec((tm, d), lambda i: (i, 0)),
        out_shape=jax.ShapeDtypeStruct((t, d), F32),
        scratch_shapes=[pltpu.VMEM((tm, d), F32), pltpu.VMEM((tm, d), F32),
                        pltpu.SemaphoreType.DMA((2,))],
        compiler_params=_params("arbitrary"),
        name="moe_combine_norm",
    )(dest1.reshape(nblk, 1, tm), dest2.reshape(nblk, 1, tm), x2d, meta,
      final_g.reshape(1, d), ys)


def moe_layer_final(x2d, norm_g, router_w, router_b, wg, wu, wd, final_g):
    t, d = x2d.shape
    rw_pad = jnp.pad(router_w, ((0, 0), (0, LANES - N_EXPERTS)))
    rb_pad = jnp.pad(router_b, (0, LANES - N_EXPERTS)).reshape(1, LANES)
    h, meta, counts = moe_router(x2d, norm_g, rw_pad, rb_pad)

    counts = counts[0, :N_EXPERTS].astype(I32)
    padded = (counts + MOE_ROWS - 1) // MOE_ROWS * MOE_ROWS
    pad_ends = jnp.cumsum(padded)
    pad_starts = pad_ends - padded
    e = meta[:, META_E:META_E + TOP_K].astype(I32)
    rank = meta[:, META_RANK:META_RANK + TOP_K].astype(I32)
    dest = pad_starts[e] + rank
    n_blk = (t * TOP_K) // MOE_ROWS + N_EXPERTS
    n_rows = n_blk * MOE_ROWS
    tok = jnp.broadcast_to(jnp.arange(t, dtype=I32)[:, None], (t, TOP_K))
    rows_tok = jnp.zeros((n_rows,), I32).at[dest.reshape(-1)].set(tok.reshape(-1))
    blk_start = jnp.arange(n_blk, dtype=I32) * MOE_ROWS
    blk_valid = (blk_start < pad_ends[-1]).astype(I32)
    blk_e = jnp.minimum(jnp.sum((blk_start[:, None] >= pad_ends[None, :]).astype(I32), axis=1),
                        N_EXPERTS - 1)

    xs = gather_rows(h, rows_tok)
    ys = grouped_ffn(xs, blk_e, blk_valid, wg, wu, wd)
    return moe_combine_norm(x2d, meta, dest[:, 0], dest[:, 1], ys, final_g)


def _col_slices():
    sizes = (('moba', 3 * MIX), ('gla_qkv', 2 * GLA_DK + GLA_DV), ('gla_a', GLA_GATE_RANK),
             ('gla_r', GLA_DV), ('mlstm_qkv', 2 * MLSTM_DQK + MLSTM_DV),
             ('mlstm_if', 2 * MLSTM_HEADS), ('mlstm_o', MLSTM_DV),
             ('merge_gate', N_BRANCHES * D_MODEL))
    out, off = {}, 0
    for name, size in sizes:
        out[name] = slice(off, off + size)
        off += size
    return out


def _pad_cols(a, width):
    return jnp.pad(a, ((0, 0), (0, width - a.shape[1])))


def hybrid_mixer_layer(x2d, b, s, norm_g, w_in, b_in, conv_w, conv_b, gate_w, gate_b, gla_norm_g,
                       w_up_moba, w_up_gla, w_up_mlstm, w_o, tables):
    cs = _col_slices()
    bias = b_in.reshape(1, -1)
    wcat = lambda names: jnp.concatenate([w_in[:, cs[n]] for n in names], axis=1)
    bcat = lambda names: jnp.concatenate([bias[:, cs[n]] for n in names], axis=1)

    q, k, v, km = moba_proj(x2d, norm_g, w_in[:, cs['moba']].astype(BF16), bias[:, cs['moba']],
                            tables, s, tm=512)
    y_moba = moba_attention(q.reshape(b, s, MIX), k.reshape(b, s, MIX), v.reshape(b, s, MIX),
                            km.reshape(b, s // MOBA_BLOCK, MIX)).reshape(b * s, MIX)

    w_gla = jnp.concatenate([wcat(['gla_qkv', 'gla_r']), _pad_cols(w_in[:, cs['gla_a']], LANES)], axis=1)
    b_gla = jnp.concatenate([bcat(['gla_qkv', 'gla_r']), _pad_cols(bias[:, cs['gla_a']], LANES)], axis=1)
    z_gla = norm_matmul(x2d, norm_g, w_gla.astype(BF16), b_gla, tm=512)
    gate_w_pad = jnp.pad(gate_w, ((0, LANES - GLA_GATE_RANK), (0, 0)))
    y_gla = gla_scan(z_gla, gate_w_pad, gate_b, gla_norm_g, b, s)

    w_ml = jnp.concatenate([wcat(['mlstm_qkv', 'mlstm_o']), _pad_cols(w_in[:, cs['mlstm_if']], LANES)], axis=1)
    b_ml = jnp.concatenate([bcat(['mlstm_qkv', 'mlstm_o']), _pad_cols(bias[:, cs['mlstm_if']], LANES)], axis=1)
    z_ml = norm_matmul(x2d, norm_g, w_ml.astype(BF16), b_ml, tm=512)
    n_if = 2 * MLSTM_HEADS
    ifr = z_ml[:, 4 * MIX:4 * MIX + n_if].reshape(b * s // CHUNK, CHUNK, n_if).transpose(0, 2, 1)
    y_ml = mlstm_scan(z_ml, ifr, conv_w, conv_b, b, s)

    return merge_out(x2d, norm_g, w_in[:, cs['merge_gate']].astype(BF16), bias[:, cs['merge_gate']],
                     y_moba, y_gla, y_ml, w_up_moba.astype(BF16), w_up_gla.astype(BF16),
                     w_up_mlstm.astype(BF16), w_o.astype(BF16), tm=512)


def kernel(x, attn_norm_g, w_in, b_in, mlstm_conv_w, mlstm_conv_b, gla_gate_w, gla_gate_b, gla_norm_g, w_up_moba, w_up_gla, w_up_mlstm, w_o, ffn_norm_g, ffn_w_gate, ffn_w_up, ffn_w_down, router_w, router_b, moe_w_gate, moe_w_up, moe_w_down, final_norm_g):
    b, s, d = x.shape
    depth = w_in.shape[0]
    assert d == D_MODEL and depth == 2, "two layers: dense SwiGLU then MoE"
    assert s % SCAN_ROWS == 0 and s % MOBA_BLOCK == 0
    x2d = x.reshape(b * s, d)
    tables = _rope_tables(s)
    for l in range(depth):
        x2d = hybrid_mixer_layer(x2d, b, s, attn_norm_g[l], w_in[l], b_in[l], mlstm_conv_w[l],
                                 mlstm_conv_b[l], gla_gate_w[l], gla_gate_b[l], gla_norm_g[l],
                                 w_up_moba[l], w_up_gla[l], w_up_mlstm[l], w_o[l], tables)
        j = l // 2
        if l % 2 == 0:
            x2d = dense_ffn(x2d, ffn_norm_g[l], ffn_w_gate[j].astype(BF16), ffn_w_up[j].astype(BF16),
                            ffn_w_down[j].astype(BF16), tm=512)
        else:
            x2d = moe_layer_final(x2d, ffn_norm_g[l], router_w[j], router_b[j],
                                  moe_w_gate[j].astype(BF16), moe_w_up[j].astype(BF16),
                                  moe_w_down[j].astype(BF16), final_norm_g)
    return x2d.reshape(b, s, d)
```

```python
import functools

import jax
import jax.numpy as jnp
from jax import lax
from jax.experimental import pallas as pl
from jax.experimental.pallas import tpu as pltpu

F32 = jnp.float32
BF16 = jnp.bfloat16
I32 = jnp.int32

D_MODEL = 1024
MIX = D_MODEL // 2
MOBA_HEAD_DIM = 64
MOBA_BLOCK = 256
MOBA_TOPK = 3
ROPE_THETA = 500000.0
ROPE_DIMS = MOBA_HEAD_DIM // 4
GLA_HEADS = 4
GLA_DK = MIX // 2
GLA_DV = MIX
GLA_GATE_RANK = 16
GLA_TAU = 16.0
MLSTM_HEADS = 4
MLSTM_DQK = MIX
MLSTM_DV = MIX
MLSTM_CONV = 4
CHUNK = 64
N_BRANCHES = 3
N_EXPERTS = 8
TOP_K = 2
EPS = 1e-6
LOG2E = 1.4426950408889634

LANES = 128
VMEM_LIMIT = 56 * 1024 * 1024

NN = (((1,), (0,)), ((), ()))
NT = (((1,), (1,)), ((), ()))
TN = (((0,), (0,)), ((), ()))


def _dot(a, b, dims=NN):
    return lax.dot_general(a, b, dims, preferred_element_type=F32)


def _dot_f32(a, b, dims=NN):
    return lax.dot_general(a, b, dims, preferred_element_type=F32,
                           precision=lax.Precision.HIGHEST)


def _rms(x, g):
    return x * lax.rsqrt(jnp.mean(x * x, axis=-1, keepdims=True) + EPS) * g


def _sigmoid(x):
    return 1.0 / (1.0 + jnp.exp(-x))


def _log_sigmoid(x):
    return jnp.minimum(x, 0.0) - jnp.log(1.0 + jnp.exp(-jnp.abs(x)))


def _params(*sem):
    return pltpu.CompilerParams(dimension_semantics=sem, vmem_limit_bytes=VMEM_LIMIT)


def _resident(shape):
    n = len(shape)
    return pl.BlockSpec(shape, lambda *_: (0,) * n, pipeline_mode=pl.Buffered(1))


def _norm_mm_kernel(x_ref, g_ref, w_ref, b_ref, o_ref):
    h = _rms(x_ref[...], g_ref[...]).astype(BF16)
    o_ref[...] = (_dot(h, w_ref[...]) + b_ref[...]).astype(o_ref.dtype)


def norm_matmul(x2d, g, w, b, *, tm, out_dtype=F32):
    t, d = x2d.shape
    n = w.shape[1]
    return pl.pallas_call(
        _norm_mm_kernel,
        grid=(t // tm,),
        in_specs=[pl.BlockSpec((tm, d), lambda i: (i, 0)), _resident((1, d)),
                  _resident((d, n)), _resident((1, n))],
        out_specs=pl.BlockSpec((tm, n), lambda i: (i, 0)),
        out_shape=jax.ShapeDtypeStruct((t, n), out_dtype),
        compiler_params=_params("parallel"),
        name="norm_matmul",
    )(x2d, g.reshape(1, d), w, b.reshape(1, n))


def _rope_tables(s):
    half = ROPE_DIMS // 2
    inv_freq = jnp.power(ROPE_THETA, -jnp.arange(half, dtype=F32) * 2.0 / ROPE_DIMS)
    ang = jnp.arange(s).astype(F32)[:, None] * inv_freq[None, :]
    cos, sin = jnp.cos(ang), jnp.sin(ang)
    ones = jnp.ones((s, MOBA_HEAD_DIM - ROPE_DIMS), F32)
    zeros = jnp.zeros((s, MOBA_HEAD_DIM - ROPE_DIMS), F32)
    zh = jnp.zeros((s, half), F32)
    c = jnp.concatenate([cos, cos, ones], axis=1)
    sa = jnp.concatenate([zh, sin, zeros], axis=1)
    sb = jnp.concatenate([-sin, zh, zeros], axis=1)
    rep = LANES // MOBA_HEAD_DIM
    return tuple(jnp.tile(a, (1, rep)) for a in (c, sa, sb))


def _moba_proj_kernel(x_ref, g_ref, w_ref, b_ref, c_ref, sa_ref, sb_ref,
                      q_ref, k_ref, vt_ref, km_ref):
    h = _rms(x_ref[...], g_ref[...]).astype(BF16)
    z = _dot(h, w_ref[...]) + b_ref[...]
    tm = z.shape[0]
    rep = MIX // LANES
    c = jnp.concatenate([c_ref[...]] * rep, axis=1)
    sa = jnp.concatenate([sa_ref[...]] * rep, axis=1)
    sb = jnp.concatenate([sb_ref[...]] * rep, axis=1)
    half = ROPE_DIMS // 2

    def rot(t):
        return t * c + pltpu.roll(t, half, 1) * sa + pltpu.roll(t, MIX - half, 1) * sb

    q = rot(z[:, :MIX]) * (MOBA_HEAD_DIM ** -0.5 * LOG2E)
    k = rot(z[:, MIX:2 * MIX])
    q_ref[...] = q.astype(BF16)
    k_ref[...] = k.astype(BF16)
    nb = tm // MOBA_BLOCK
    for i in range(nb):
        rows = slice(i * MOBA_BLOCK, (i + 1) * MOBA_BLOCK)
        vt_ref[i] = z[rows, 2 * MIX:].T.astype(BF16)
        km_ref[i] = jnp.mean(k[rows, :], axis=0, keepdims=True)


def moba_proj(x2d, g, w, b, tables, s, *, tm):
    t, d = x2d.shape
    n = w.shape[1]
    c, sa, sb = tables
    spt = s // tm
    tab = pl.BlockSpec((tm, LANES), lambda i: (i % spt, 0))
    row = pl.BlockSpec((tm, MIX), lambda i: (i, 0))
    nb = tm // MOBA_BLOCK
    return pl.pallas_call(
        _moba_proj_kernel,
        grid=(t // tm,),
        in_specs=[pl.BlockSpec((tm, d), lambda i: (i, 0)), _resident((1, d)),
                  _resident((d, n)), _resident((1, n)), tab, tab, tab],
        out_specs=[row, row, pl.BlockSpec((nb, MIX, MOBA_BLOCK), lambda i: (i, 0, 0)),
                   pl.BlockSpec((nb, 1, MIX), lambda i: (i, 0, 0))],
        out_shape=[jax.ShapeDtypeStruct((t, MIX), BF16)] * 2
        + [jax.ShapeDtypeStruct((t // MOBA_BLOCK, MIX, MOBA_BLOCK), BF16),
           jax.ShapeDtypeStruct((t // MOBA_BLOCK, 1, MIX), F32)],
        compiler_params=_params("parallel"),
        name="moba_proj",
    )(x2d, g.reshape(1, d), w, b.reshape(1, n), c, sa, sb)


MOBA_TQ = MOBA_BLOCK
HEADS_PER_STEP = LANES // MOBA_HEAD_DIM


def _moba_attn_kernel(q_ref, k_ref, vt_ref, km_ref, o_ref, bias_ref, qh_ref, s_ref, p_ref):
    cur = pl.program_id(2)
    nb = km_ref.shape[1]
    hd = MOBA_HEAD_DIM
    neg = -jnp.inf
    own = pl.multiple_of(cur * MOBA_BLOCK, MOBA_BLOCK)
    nh = HEADS_PER_STEP

    def v_rows(blk, hh):
        return vt_ref[0, blk, hh * hd:(hh + 1) * hd, :]

    def scores_into(slot, blk):
        off = pl.multiple_of(blk * MOBA_BLOCK, MOBA_BLOCK)
        kb = k_ref[0, pl.ds(off, MOBA_BLOCK), :]
        for hh in range(nh):
            s_ref[slot, hh] = _dot(kb, qh_ref[hh], NT)

    q_all = q_ref[0]
    lane = lax.broadcasted_iota(I32, q_all.shape, 1)
    km = km_ref[0]
    km_hi = km.astype(BF16)
    km_lo = (km - km_hi.astype(F32)).astype(BF16)
    k_own = k_ref[0, pl.ds(own, MOBA_BLOCK), :]
    init = []
    for hh in range(nh):
        q = jnp.where((lane >= hh * hd) & (lane < (hh + 1) * hd), q_all, jnp.zeros_like(q_all))
        qh_ref[hh] = q
        gate = _dot(km_hi, q, NT) + _dot(km_lo, q, NT)
        blk = lax.broadcasted_iota(I32, gate.shape, 0)
        gate = jnp.where(blk < cur, gate, neg)
        sel = jnp.zeros(gate.shape, F32)
        for _ in range(MOBA_TOPK):
            m = jnp.max(gate, axis=0, keepdims=True)
            first = jnp.min(jnp.where(gate == m, blk, nb), axis=0, keepdims=True)
            first = jnp.where(m > neg, first, nb)
            pick = blk == first
            sel = jnp.where(pick, 1.0, sel)
            gate = jnp.where(pick, neg, gate)
        bias_ref[hh] = jnp.where(sel > 0.0, 0.0, neg)

        s = _dot(k_own, q, NT)
        k_pos = lax.broadcasted_iota(I32, s.shape, 0)
        q_pos = lax.broadcasted_iota(I32, s.shape, 1)
        s = jnp.where(k_pos <= q_pos, s, neg)
        m0 = jnp.max(s, axis=0, keepdims=True)
        p = jnp.exp2(s - m0)
        p_ref[hh] = p.astype(BF16)
        init.append((m0, jnp.sum(p, axis=0, keepdims=True), jnp.zeros((hd, MOBA_TQ), F32),
                     jnp.ones((1, MOBA_TQ), F32)))
    scores_into(0, 0)

    def step(j, slot, carry):
        pend = jnp.where(j == 0, cur, j - 1)
        pv = [_dot(v_rows(pend, hh), p_ref[hh]) for hh in range(nh)]
        out = []
        for hh in range(nh):
            m_i, l_i, acc, a_pend = carry[hh]
            s = s_ref[slot, hh] + bias_ref[hh, pl.ds(j, 1), :]
            m_new = jnp.maximum(m_i, jnp.max(s, axis=0, keepdims=True))
            alpha = jnp.exp2(m_i - m_new)
            p = jnp.exp2(s - m_new)
            l_new = alpha * l_i + jnp.sum(p, axis=0, keepdims=True)
            acc_new = a_pend * acc + pv[hh]
            p_ref[hh] = p.astype(BF16)
            out.append((m_new, l_new, acc_new, alpha))
        scores_into(1 - slot, jnp.minimum(j + 1, nb - 1))
        return tuple(out)

    def body(i, carry):
        return step(2 * i + 1, 1, step(2 * i, 0, carry))

    trips = (cur + 1) // 2
    final = lax.fori_loop(0, trips, body, tuple(init))
    pend = jnp.where(trips == 0, cur, 2 * trips - 1)
    outs = []
    for hh in range(nh):
        _, l_f, acc, a_pend = final[hh]
        acc = a_pend * acc + _dot(v_rows(pend, hh), p_ref[hh])
        outs.append((acc / l_f).T)
    o_ref[0] = jnp.concatenate(outs, axis=1).astype(o_ref.dtype)


def moba_attention(q, k, vt, km):
    b, s, _ = q.shape
    nb = km.shape[1]
    qspec = pl.BlockSpec((1, MOBA_TQ, LANES), lambda bi, hp, qi: (bi, qi, hp))
    return pl.pallas_call(
        _moba_attn_kernel,
        grid=(b, MIX // LANES, s // MOBA_TQ),
        in_specs=[qspec,
                  pl.BlockSpec((1, s, LANES), lambda bi, hp, qi: (bi, 0, hp)),
                  pl.BlockSpec((1, nb, LANES, MOBA_BLOCK), lambda bi, hp, qi: (bi, 0, hp, 0)),
                  pl.BlockSpec((1, nb, LANES), lambda bi, hp, qi: (bi, 0, hp))],
        out_specs=qspec,
        out_shape=jax.ShapeDtypeStruct((b, s, MIX), BF16),
        scratch_shapes=[pltpu.VMEM((HEADS_PER_STEP, nb, MOBA_TQ), F32),
                        pltpu.VMEM((HEADS_PER_STEP, MOBA_TQ, LANES), BF16),
                        pltpu.VMEM((2, HEADS_PER_STEP, MOBA_BLOCK, MOBA_TQ), F32),
                        pltpu.VMEM((HEADS_PER_STEP, MOBA_BLOCK, MOBA_TQ), BF16)],
        compiler_params=_params("parallel", "parallel", "arbitrary"),
        name="moba_attention",
    )(q, k, vt, km)


SCAN_ROWS = 512


def _gla_kernel(qk_ref, v_ref, r_ref, a_ref, gw_ref, gb_ref, ng_ref, o_ref, st_ref):
    @pl.when(pl.program_id(1) == 0)
    def _():
        st_ref[...] = jnp.zeros_like(st_ref)

    dk = GLA_DK // GLA_HEADS
    dv = GLA_DV // GLA_HEADS
    L = CHUNK
    ri = lax.broadcasted_iota(I32, (L, L), 0)
    ci = lax.broadcasted_iota(I32, (L, L), 1)
    causal = ci <= ri
    tri = jnp.where(causal, 1.0, 0.0).astype(F32)
    gw = gw_ref[...]
    gb = gb_ref[...]
    ng = ng_ref[...]

    def chunk(c, _):
        rows = pl.ds(pl.multiple_of(c * L, L), L)
        pre = _dot_f32(a_ref[rows, :], gw) + gb
        log_a = _log_sigmoid(pre) / GLA_TAU
        cum = _dot_f32(tri, log_a)
        cum_last = cum[L - 1:L, :]
        qk = qk_ref[rows, :]
        q = qk[:, :GLA_DK] * (dk ** -0.5)
        k = qk[:, GLA_DK:]
        q_dec = (q * jnp.exp(cum)).astype(BF16)
        k_inv = (k * jnp.exp(-cum)).astype(BF16)
        k_end = (k * jnp.exp(cum_last - cum)).astype(BF16)
        dec = jnp.exp(cum_last)
        v = v_ref[rows, :].astype(BF16)
        r = r_ref[rows, :]
        outs = []
        for h in range(GLA_HEADS):
            hs = slice(h * dk, (h + 1) * dk)
            vs = slice(h * dv, (h + 1) * dv)
            st = st_ref[h]
            att = jnp.where(causal, _dot(q_dec[:, hs], k_inv[:, hs], NT), 0.0)
            o = _dot(att.astype(BF16), v[:, vs]) + _dot(q_dec[:, hs], st.astype(BF16), NT)
            st_ref[h] = st * dec[:, hs] + _dot(v[:, vs], k_end[:, hs], TN)
            o = o * lax.rsqrt(jnp.mean(o * o, axis=-1, keepdims=True) + EPS)
            outs.append(o)
        o_all = jnp.concatenate(outs, axis=1)
        o_ref[rows, :] = (o_all * ng * (r * _sigmoid(r))).astype(o_ref.dtype)
        return 0

    lax.fori_loop(0, SCAN_ROWS // L, chunk, 0)


def gla_scan(z, gate_w_pad, gate_b, norm_g, b, s):
    t = z.shape[0]
    nc = s // SCAN_ROWS
    wide = lambda col: pl.BlockSpec((SCAN_ROWS, MIX), lambda bi, c: (bi * nc + c, col))
    dk = GLA_DK // GLA_HEADS
    dv = GLA_DV // GLA_HEADS
    return pl.pallas_call(
        _gla_kernel,
        grid=(b, nc),
        in_specs=[wide(0), wide(1), wide(2),
                  pl.BlockSpec((SCAN_ROWS, LANES), lambda bi, c: (bi * nc + c, 3 * MIX // LANES)),
                  _resident((LANES, GLA_DK)), _resident((1, GLA_DK)), _resident((1, GLA_DV))],
        out_specs=pl.BlockSpec((SCAN_ROWS, MIX), lambda bi, c: (bi * nc + c, 0)),
        out_shape=jax.ShapeDtypeStruct((t, MIX), BF16),
        scratch_shapes=[pltpu.VMEM((GLA_HEADS, dv, dk), F32)],
        compiler_params=_params("parallel", "arbitrary"),
        name="gla_scan",
    )(z, z, z, z, gate_w_pad, gate_b.reshape(1, GLA_DK), norm_g.reshape(1, GLA_DV))


TAIL = 8


def _mlstm_kernel(qk_ref, v_ref, og_ref, ifc_ref, ifr_ref, cw_ref, cb_ref, o_ref,
                  tail_ref, ct_ref, n_ref, m_ref, qkc_ref):
    @pl.when(pl.program_id(1) == 0)
    def _():
        tail_ref[...] = jnp.zeros_like(tail_ref)
        ct_ref[...] = jnp.zeros_like(ct_ref)
        n_ref[...] = jnp.zeros_like(n_ref)
        m_ref[...] = jnp.zeros_like(m_ref)

    nh = MLSTM_HEADS
    dk = MLSTM_DQK // nh
    dv = MLSTM_DV // nh
    L = CHUNK
    rows_n = qk_ref.shape[0]

    x = qk_ref[...]
    xx = jnp.concatenate([tail_ref[...], x], axis=0)
    cw = cw_ref[...]
    y = cb_ref[...] + sum(
        cw[w:w + 1, :] * xx[TAIL - (MLSTM_CONV - 1) + w:TAIL - (MLSTM_CONV - 1) + w + rows_n, :]
        for w in range(MLSTM_CONV))
    tail_ref[...] = x[rows_n - TAIL:, :]
    qkc_ref[...] = y * _sigmoid(y)

    ri = lax.broadcasted_iota(I32, (L, L), 0)
    ci = lax.broadcasted_iota(I32, (L, L), 1)
    causal = ci <= ri
    tri = jnp.where(causal, 1.0, 0.0).astype(F32)
    tri_t = jnp.where(ri <= ci, 1.0, 0.0).astype(F32)
    neg = -jnp.inf

    def chunk(c, _):
        rows = pl.ds(pl.multiple_of(c * L, L), L)
        qkc = qkc_ref[rows, :]
        q = qkc[:, :MLSTM_DQK].astype(BF16)
        kf = qkc[:, MLSTM_DQK:] * (dk ** -0.5)
        k = kf.astype(BF16)
        v = v_ref[rows, :].astype(BF16)
        ifc = ifc_ref[rows, :]
        ifr = ifr_ref[c]
        cum_c = _dot_f32(tri, _log_sigmoid(ifc))
        cum_r = _dot_f32(_log_sigmoid(ifr), tri_t)
        outs = []
        for h in range(nh):
            hs = slice(h * dk, (h + 1) * dk)
            vs = slice(h * dv, (h + 1) * dv)
            cum_i = cum_c[:, nh + h:nh + h + 1]
            i_i = ifc[:, h:h + 1]
            cum_j = cum_r[nh + h:nh + h + 1, :]
            i_j = ifr[h:h + 1, :]
            m_prev = m_ref[h:h + 1, 0:1]
            ct = ct_ref[h]
            n_prev = n_ref[h:h + 1, :]
            cum_last = cum_i[L - 1:L, :]

            log_d = jnp.where(causal, cum_i - cum_j + i_j, neg)
            log_inter = cum_i + m_prev
            m_i = jnp.maximum(log_inter, jnp.max(log_d, axis=1, keepdims=True))
            w_intra = _dot(q[:, hs], k[:, hs], NT) * jnp.exp(log_d - m_i)
            w_inter = jnp.exp(log_inter - m_i)
            num = (w_inter * _dot(q[:, hs], ct.astype(BF16), NT)
                   + _dot(w_intra.astype(BF16), v[:, vs]))
            qn = jnp.sum(qkc[:, hs] * n_prev, axis=1, keepdims=True)
            den = w_inter * qn + jnp.sum(w_intra, axis=1, keepdims=True)
            outs.append(num / jnp.maximum(jnp.abs(den), jnp.exp(-m_i)))

            a_end = cum_last - cum_i + i_i
            m_new = jnp.maximum(cum_last + m_prev, jnp.max(a_end, axis=0, keepdims=True))
            w = jnp.exp(a_end - m_new)
            dec = jnp.exp(cum_last + m_prev - m_new)
            wk = w * kf[:, hs]
            ct_ref[h] = dec * ct + _dot(v[:, vs], wk.astype(BF16), TN)
            n_ref[h:h + 1, :] = dec * n_prev + jnp.sum(wk, axis=0, keepdims=True)
            m_ref[h:h + 1, :] = jnp.broadcast_to(m_new, (1, LANES))
        og = og_ref[rows, :]
        o_ref[rows, :] = (jnp.concatenate(outs, axis=1) * _sigmoid(og)).astype(o_ref.dtype)
        return 0

    lax.fori_loop(0, rows_n // L, chunk, 0)


def mlstm_scan(z, ifr, conv_w, conv_b, b, s):
    t = z.shape[0]
    nc = s // SCAN_ROWS
    cpt = SCAN_ROWS // CHUNK
    dk = MLSTM_DQK // MLSTM_HEADS
    dv = MLSTM_DV // MLSTM_HEADS
    row = lambda w, col: pl.BlockSpec((SCAN_ROWS, w), lambda bi, c: (bi * nc + c, col))
    return pl.pallas_call(
        _mlstm_kernel,
        grid=(b, nc),
        in_specs=[row(2 * MLSTM_DQK, 0), row(MIX, 2), row(MIX, 3),
                  row(LANES, 4 * MIX // LANES),
                  pl.BlockSpec((cpt, 2 * MLSTM_HEADS, CHUNK), lambda bi, c: (bi * nc + c, 0, 0)),
                  _resident((MLSTM_CONV, 2 * MLSTM_DQK)), _resident((1, 2 * MLSTM_DQK))],
        out_specs=row(MIX, 0),
        out_shape=jax.ShapeDtypeStruct((t, MIX), BF16),
        scratch_shapes=[pltpu.VMEM((TAIL, 2 * MLSTM_DQK), F32),
                        pltpu.VMEM((MLSTM_HEADS, dv, dk), F32),
                        pltpu.VMEM((8, dk), F32),
                        pltpu.VMEM((8, LANES), F32),
                        pltpu.VMEM((SCAN_ROWS, 2 * MLSTM_DQK), F32)],
        compiler_params=_params("parallel", "arbitrary"),
        name="mlstm_scan",
    )(z, z, z, z, ifr, conv_w, conv_b.reshape(1, -1))


def _merge_kernel(x_ref, g_ref, wg_ref, bg_ref, ya_ref, yb_ref, yc_ref,
                  wa_ref, wb_ref, wc_ref, wo_ref, o_ref):
    x = x_ref[...]
    h = _rms(x, g_ref[...]).astype(BF16)
    d = D_MODEL
    merged = None
    for i, (y_ref, w_ref) in enumerate(((ya_ref, wa_ref), (yb_ref, wb_ref), (yc_ref, wc_ref))):
        gate = _sigmoid(_dot(h, wg_ref[:, i * d:(i + 1) * d]) + bg_ref[:, i * d:(i + 1) * d])
        term = gate * _dot(y_ref[...], w_ref[...])
        merged = term if merged is None else merged + term
    o_ref[...] = x + _dot(merged.astype(BF16), wo_ref[...])


def merge_out(x2d, g, w_gate, b_gate, ya, yb, yc, wa, wb, wc, wo, *, tm):
    t, d = x2d.shape
    row = lambda w: pl.BlockSpec((tm, w), lambda i: (i, 0))
    return pl.pallas_call(
        _merge_kernel,
        grid=(t // tm,),
        in_specs=[row(d), _resident((1, d)), _resident((d, N_BRANCHES * d)),
                  _resident((1, N_BRANCHES * d)), row(MIX), row(MIX), row(MIX),
                  _resident((MIX, d)), _resident((MIX, d)), _resident((MIX, d)),
                  _resident((d, d))],
        out_specs=row(d),
        out_shape=jax.ShapeDtypeStruct((t, d), F32),
        compiler_params=_params("parallel"),
        name="merge_out",
    )(x2d, g.reshape(1, d), w_gate, b_gate.reshape(1, -1), ya, yb, yc, wa, wb, wc, wo)


def _ffn_kernel(x_ref, g_ref, wg_ref, wu_ref, wd_ref, o_ref, h_ref, acc_ref):
    f = pl.program_id(1)

    @pl.when(f == 0)
    def _():
        h_ref[...] = _rms(x_ref[...], g_ref[...]).astype(BF16)
        acc_ref[...] = jnp.zeros_like(acc_ref)

    h = h_ref[...]
    gt = _dot(h, wg_ref[...])
    up = _dot(h, wu_ref[...])
    act = (gt * _sigmoid(gt) * up).astype(BF16)
    acc_ref[...] += _dot(act, wd_ref[...])

    @pl.when(f == pl.num_programs(1) - 1)
    def _():
        o_ref[...] = x_ref[...] + acc_ref[...]


def _ff_tile(ff):
    for cand in (1408, 1024, 768, 512, 256, 128):
        if ff % cand == 0:
            return cand
    raise ValueError(f"unsupported d_ff {ff}")


def dense_ffn(x2d, g, wg, wu, wd, *, tm):
    t, d = x2d.shape
    ff = wg.shape[1]
    tf = _ff_tile(ff)
    return pl.pallas_call(
        _ffn_kernel,
        grid=(t // tm, ff // tf),
        in_specs=[pl.BlockSpec((tm, d), lambda i, f: (i, 0)), _resident((1, d)),
                  pl.BlockSpec((d, tf), lambda i, f: (0, f)),
                  pl.BlockSpec((d, tf), lambda i, f: (0, f)),
                  pl.BlockSpec((tf, d), lambda i, f: (f, 0))],
        out_specs=pl.BlockSpec((tm, d), lambda i, f: (i, 0)),
        out_shape=jax.ShapeDtypeStruct((t, d), F32),
        scratch_shapes=[pltpu.VMEM((tm, d), BF16), pltpu.VMEM((tm, d), F32)],
        compiler_params=_params("parallel", "arbitrary"),
        name="dense_ffn",
    )(x2d, g.reshape(1, d), wg, wu, wd)


ROUTER_TM = 512
MOE_ROWS = 512
META_E, META_G, META_RANK = 0, 2, 4


def _router_kernel(x_ref, g_ref, rw_ref, rb_ref, h_ref, meta_ref, cnt_ref, carry_ref):
    @pl.when(pl.program_id(0) == 0)
    def _():
        carry_ref[...] = jnp.zeros_like(carry_ref)

    h = _rms(x_ref[...], g_ref[...])
    h_ref[...] = h
    tm = h.shape[0]
    lane = lax.broadcasted_iota(I32, (tm, LANES), 1)
    neg = -jnp.inf
    logits = jnp.where(lane < N_EXPERTS, _dot_f32(h, rw_ref[...]) + rb_ref[...], neg)
    m1 = jnp.max(logits, axis=1, keepdims=True)
    e1 = jnp.min(jnp.where(logits == m1, lane, LANES), axis=1, keepdims=True)
    rest = jnp.where(lane == e1, neg, logits)
    m2 = jnp.max(rest, axis=1, keepdims=True)
    e2 = jnp.min(jnp.where(rest == m2, lane, LANES), axis=1, keepdims=True)
    ex = jnp.exp(m2 - m1)
    g1 = 1.0 / (1.0 + ex)
    g2 = ex / (1.0 + ex)
    oh1 = jnp.where(lane == e1, 1.0, 0.0)
    oh2 = jnp.where(lane == e2, 1.0, 0.0)
    cnt = oh1 + oh2
    ri = lax.broadcasted_iota(I32, (tm, tm), 0)
    ci = lax.broadcasted_iota(I32, (tm, tm), 1)
    strict = jnp.where(ci < ri, 1.0, 0.0).astype(BF16)
    excl = _dot(strict, cnt.astype(BF16)) + carry_ref[...]
    r1 = jnp.sum(excl * oh1, axis=1, keepdims=True)
    r2 = jnp.sum(excl * oh2, axis=1, keepdims=True)
    carry_ref[...] += jnp.sum(cnt, axis=0, keepdims=True)
    cnt_ref[...] = carry_ref[...]
    meta = jnp.zeros((tm, LANES), F32)
    for pos, val in ((META_E, e1.astype(F32)), (META_E + 1, e2.astype(F32)),
                     (META_G, g1), (META_G + 1, g2), (META_RANK, r1), (META_RANK + 1, r2)):
        meta = jnp.where(lane == pos, val, meta)
    meta_ref[...] = meta


def moe_router(x2d, g, rw_pad, rb_pad):
    t, d = x2d.shape
    tm = ROUTER_TM
    return pl.pallas_call(
        _router_kernel,
        grid=(t // tm,),
        in_specs=[pl.BlockSpec((tm, d), lambda i: (i, 0)), _resident((1, d)),
                  _resident((d, LANES)), _resident((1, LANES))],
        out_specs=[pl.BlockSpec((tm, d), lambda i: (i, 0)),
                   pl.BlockSpec((tm, LANES), lambda i: (i, 0)),
                   pl.BlockSpec((1, LANES), lambda i: (0, 0))],
        out_shape=[jax.ShapeDtypeStruct((t, d), F32), jax.ShapeDtypeStruct((t, LANES), F32),
                   jax.ShapeDtypeStruct((1, LANES), F32)],
        scratch_shapes=[pltpu.VMEM((1, LANES), F32)],
        compiler_params=_params("arbitrary"),
        name="moe_router",
    )(x2d, g.reshape(1, d), rw_pad, rb_pad)


GATHER_ROWS = 256


def _row_copy(src_hbm, dst_ref, src_row, dst_row, sem):
    return pltpu.make_async_copy(src_hbm.at[pl.ds(src_row, 1)], dst_ref.at[pl.ds(dst_row, 1)], sem)


def _gather_kernel(idx_ref, src_hbm, o_ref, sem):
    n = o_ref.shape[0]

    def start(r, _):
        _row_copy(src_hbm, o_ref, idx_ref[0, 0, r], r, sem).start()
        return 0

    def wait(r, _):
        _row_copy(src_hbm, o_ref, 0, r, sem).wait()
        return 0

    lax.fori_loop(0, n, start, 0)
    lax.fori_loop(0, n, wait, 0)


def gather_rows(src, idx):
    n = idx.shape[0]
    d = src.shape[1]
    nblk = n // GATHER_ROWS
    return pl.pallas_call(
        _gather_kernel,
        grid=(nblk,),
        in_specs=[pl.BlockSpec((1, 1, GATHER_ROWS), lambda i: (i, 0, 0), memory_space=pltpu.SMEM),
                  pl.BlockSpec(memory_space=pl.ANY)],
        out_specs=pl.BlockSpec((GATHER_ROWS, d), lambda i: (i, 0)),
        out_shape=jax.ShapeDtypeStruct((n, d), src.dtype),
        scratch_shapes=[pltpu.SemaphoreType.DMA(())],
        compiler_params=_params("arbitrary"),
        name="gather_rows",
    )(idx.reshape(nblk, 1, GATHER_ROWS), src)


def _grouped_ffn_kernel(be_ref, bv_ref, x_ref, wg_ref, wu_ref, wd_ref, o_ref, acc_ref):
    i = pl.program_id(0)
    f = pl.program_id(1)

    @pl.when(f == 0)
    def _():
        acc_ref[...] = jnp.zeros_like(acc_ref)

    @pl.when(bv_ref[i] > 0)
    def _():
        h = x_ref[...].astype(BF16)
        gt = _dot(h, wg_ref[...])
        up = _dot(h, wu_ref[...])
        act = (gt * _sigmoid(gt) * up).astype(BF16)
        acc_ref[...] += _dot(act, wd_ref[...])

    @pl.when(f == pl.num_programs(1) - 1)
    def _():
        o_ref[...] = acc_ref[...]


def grouped_ffn(xs, blk_e, blk_valid, wg, wu, wd):
    n, d = xs.shape
    ff = wg.shape[2]
    tf = _ff_tile(ff)
    nblk = n // MOE_ROWS
    grid_spec = pltpu.PrefetchScalarGridSpec(
        num_scalar_prefetch=2,
        grid=(nblk, ff // tf),
        in_specs=[pl.BlockSpec((MOE_ROWS, d), lambda i, f, be, bv: (i, 0)),
                  pl.BlockSpec((None, d, tf), lambda i, f, be, bv: (be[i], 0, f)),
                  pl.BlockSpec((None, d, tf), lambda i, f, be, bv: (be[i], 0, f)),
                  pl.BlockSpec((None, tf, d), lambda i, f, be, bv: (be[i], f, 0))],
        out_specs=pl.BlockSpec((MOE_ROWS, d), lambda i, f, be, bv: (i, 0)),
        scratch_shapes=[pltpu.VMEM((MOE_ROWS, d), F32)],
    )
    return pl.pallas_call(
        _grouped_ffn_kernel,
        grid_spec=grid_spec,
        out_shape=jax.ShapeDtypeStruct((n, d), F32),
        compiler_params=_params("parallel", "arbitrary"),
        name="grouped_ffn",
    )(blk_e, blk_valid, xs, wg, wu, wd)


COMBINE_TM = 256


def _combine_kernel(d1_ref, d2_ref, x_ref, meta_ref, fg_ref, ys_hbm, o_ref, b1_ref, b2_ref, sem):
    n = x_ref.shape[0]

    def start(r, _):
        _row_copy(ys_hbm, b1_ref, d1_ref[0, 0, r], r, sem.at[0]).start()
        _row_copy(ys_hbm, b2_ref, d2_ref[0, 0, r], r, sem.at[1]).start()
        return 0

    def wait(r, _):
        _row_copy(ys_hbm, b1_ref, 0, r, sem.at[0]).wait()
        _row_copy(ys_hbm, b2_ref, 0, r, sem.at[1]).wait()
        return 0

    lax.fori_loop(0, n, start, 0)
    lax.fori_loop(0, n, wait, 0)
    meta = meta_ref[...]
    g1 = meta[:, META_G:META_G + 1]
    g2 = meta[:, META_G + 1:META_G + 2]
    y = x_ref[...] + (g1 * b1_ref[...] + g2 * b2_ref[...])
    o_ref[...] = _rms(y, fg_ref[...])


def moe_combine_norm(x2d, meta, dest1, dest2, ys, final_g):
    t, d = x2d.shape
    tm = COMBINE_TM
    nblk = t // tm
    idx = pl.BlockSpec((1, 1, tm), lambda i: (i, 0, 0), memory_space=pltpu.SMEM)
    return pl.pallas_call(
        _combine_kernel,
        grid=(nblk,),
        in_specs=[idx, idx, pl.BlockSpec((tm, d), lambda i: (i, 0)),
                  pl.BlockSpec((tm, LANES), lambda i: (i, 0)), _resident((1, d)),
                  pl.BlockSpec(memory_space=pl.ANY)],
        out_specs=pl.BlockSpec((tm, d), lambda i: (i, 0)),
        out_shape=jax.ShapeDtypeStruct((t, d), F32),
        scratch_shapes=[pltpu.VMEM((tm, d), F32), pltpu.VMEM((tm, d), F32),
                        pltpu.SemaphoreType.DMA((2,))],
        compiler_params=_params("arbitrary"),
        name="moe_combine_norm",
    )(dest1.reshape(nblk, 1, tm), dest2.reshape(nblk, 1, tm), x2d, meta,
      final_g.reshape(1, d), ys)


def moe_layer_final(x2d, norm_g, router_w, router_b, wg, wu, wd, final_g):
    t, d = x2d.shape
    rw_pad = jnp.pad(router_w, ((0, 0), (0, LANES - N_EXPERTS)))
    rb_pad = jnp.pad(router_b, (0, LANES - N_EXPERTS)).reshape(1, LANES)
    h, meta, counts = moe_router(x2d, norm_g, rw_pad, rb_pad)

    counts = counts[0, :N_EXPERTS].astype(I32)
    padded = (counts + MOE_ROWS - 1) // MOE_ROWS * MOE_ROWS
    pad_ends = jnp.cumsum(padded)
    pad_starts = pad_ends - padded
    e = meta[:, META_E:META_E + TOP_K].astype(I32)
    rank = meta[:, META_RANK:META_RANK + TOP_K].astype(I32)
    dest = pad_starts[e] + rank
    n_blk = (t * TOP_K) // MOE_ROWS + N_EXPERTS
    n_rows = n_blk * MOE_ROWS
    tok = jnp.broadcast_to(jnp.arange(t, dtype=I32)[:, None], (t, TOP_K))
    rows_tok = jnp.zeros((n_rows,), I32).at[dest.reshape(-1)].set(tok.reshape(-1))
    blk_start = jnp.arange(n_blk, dtype=I32) * MOE_ROWS
    blk_valid = (blk_start < pad_ends[-1]).astype(I32)
    blk_e = jnp.minimum(jnp.sum((blk_start[:, None] >= pad_ends[None, :]).astype(I32), axis=1),
                        N_EXPERTS - 1)

    xs = gather_rows(h, rows_tok)
    ys = grouped_ffn(xs, blk_e, blk_valid, wg, wu, wd)
    return moe_combine_norm(x2d, meta, dest[:, 0], dest[:, 1], ys, final_g)


def _col_slices():
    sizes = (('moba', 3 * MIX), ('gla_qkv', 2 * GLA_DK + GLA_DV), ('gla_a', GLA_GATE_RANK),
             ('gla_r', GLA_DV), ('mlstm_qkv', 2 * MLSTM_DQK + MLSTM_DV),
             ('mlstm_if', 2 * MLSTM_HEADS), ('mlstm_o', MLSTM_DV),
             ('merge_gate', N_BRANCHES * D_MODEL))
    out, off = {}, 0
    for name, size in sizes:
        out[name] = slice(off, off + size)
        off += size
    return out


def _pad_cols(a, width):
    return jnp.pad(a, ((0, 0), (0, width - a.shape[1])))


def hybrid_mixer_layer(x2d, b, s, norm_g, w_in, b_in, conv_w, conv_b, gate_w, gate_b, gla_norm_g,
                       w_up_moba, w_up_gla, w_up_mlstm, w_o, tables):
    cs = _col_slices()
    bias = b_in.reshape(1, -1)
    wcat = lambda names: jnp.concatenate([w_in[:, cs[n]] for n in names], axis=1)
    bcat = lambda names: jnp.concatenate([bias[:, cs[n]] for n in names], axis=1)

    nb = s // MOBA_BLOCK
    q, k, vt, km = moba_proj(x2d, norm_g, w_in[:, cs['moba']].astype(BF16), bias[:, cs['moba']],
                             tables, s, tm=512)
    y_moba = moba_attention(q.reshape(b, s, MIX), k.reshape(b, s, MIX),
                            vt.reshape(b, nb, MIX, MOBA_BLOCK),
                            km.reshape(b, nb, MIX)).reshape(b * s, MIX)

    w_gla = jnp.concatenate([wcat(['gla_qkv', 'gla_r']), _pad_cols(w_in[:, cs['gla_a']], LANES)], axis=1)
    b_gla = jnp.concatenate([bcat(['gla_qkv', 'gla_r']), _pad_cols(bias[:, cs['gla_a']], LANES)], axis=1)
    z_gla = norm_matmul(x2d, norm_g, w_gla.astype(BF16), b_gla, tm=512)
    gate_w_pad = jnp.pad(gate_w, ((0, LANES - GLA_GATE_RANK), (0, 0)))
    y_gla = gla_scan(z_gla, gate_w_pad, gate_b, gla_norm_g, b, s)

    w_ml = jnp.concatenate([wcat(['mlstm_qkv', 'mlstm_o']), _pad_cols(w_in[:, cs['mlstm_if']], LANES)], axis=1)
    b_ml = jnp.concatenate([bcat(['mlstm_qkv', 'mlstm_o']), _pad_cols(bias[:, cs['mlstm_if']], LANES)], axis=1)
    z_ml = norm_matmul(x2d, norm_g, w_ml.astype(BF16), b_ml, tm=512)
    n_if = 2 * MLSTM_HEADS
    ifr = z_ml[:, 4 * MIX:4 * MIX + n_if].reshape(b * s // CHUNK, CHUNK, n_if).transpose(0, 2, 1)
    y_ml = mlstm_scan(z_ml, ifr, conv_w, conv_b, b, s)

    return merge_out(x2d, norm_g, w_in[:, cs['merge_gate']].astype(BF16), bias[:, cs['merge_gate']],
                     y_moba, y_gla, y_ml, w_up_moba.astype(BF16), w_up_gla.astype(BF16),
                     w_up_mlstm.astype(BF16), w_o.astype(BF16), tm=512)


def kernel(x, attn_norm_g, w_in, b_in, mlstm_conv_w, mlstm_conv_b, gla_gate_w, gla_gate_b, gla_norm_g, w_up_moba, w_up_gla, w_up_mlstm, w_o, ffn_norm_g, ffn_w_gate, ffn_w_up, ffn_w_down, router_w, router_b, moe_w_gate, moe_w_up, moe_w_down, final_norm_g):
    b, s, d = x.shape
    depth = w_in.shape[0]
    assert d == D_MODEL and depth == 2, "two layers: dense SwiGLU then MoE"
    assert s % SCAN_ROWS == 0 and s % MOBA_BLOCK == 0
    x2d = x.reshape(b * s, d)
    tables = _rope_tables(s)
    for l in range(depth):
        x2d = hybrid_mixer_layer(x2d, b, s, attn_norm_g[l], w_in[l], b_in[l], mlstm_conv_w[l],
                                 mlstm_conv_b[l], gla_gate_w[l], gla_gate_b[l], gla_norm_g[l],
                                 w_up_moba[l], w_up_gla[l], w_up_mlstm[l], w_o[l], tables)
        j = l // 2
        if l % 2 == 0:
            x2d = dense_ffn(x2d, ffn_norm_g[l], ffn_w_gate[j].astype(BF16), ffn_w_up[j].astype(BF16),
                            ffn_w_down[j].astype(BF16), tm=512)
        else:
            x2d = moe_layer_final(x2d, ffn_norm_g[l], router_w[j], router_b[j],
                                  moe_w_gate[j].astype(BF16), moe_w_up[j].astype(BF16),
                                  moe_w_down[j].astype(BF16), final_norm_g)
    return x2d.reshape(b, s, d)
```

```python
import functools

import jax
import jax.numpy as jnp
from jax import lax
from jax.experimental import pallas as pl
from jax.experimental.pallas import tpu as pltpu

F32 = jnp.float32
BF16 = jnp.bfloat16
I32 = jnp.int32

D_MODEL = 1024
MIX = D_MODEL // 2
MOBA_HEAD_DIM = 64
MOBA_BLOCK = 256
MOBA_TOPK = 3
ROPE_THETA = 500000.0
ROPE_DIMS = MOBA_HEAD_DIM // 4
GLA_HEADS = 4
GLA_DK = MIX // 2
GLA_DV = MIX
GLA_GATE_RANK = 16
GLA_TAU = 16.0
MLSTM_HEADS = 4
MLSTM_DQK = MIX
MLSTM_DV = MIX
MLSTM_CONV = 4
CHUNK = 64
N_BRANCHES = 3
N_EXPERTS = 8
TOP_K = 2
EPS = 1e-6
LOG2E = 1.4426950408889634

LANES = 128
VMEM_LIMIT = 56 * 1024 * 1024

NN = (((1,), (0,)), ((), ()))
NT = (((1,), (1,)), ((), ()))
TN = (((0,), (0,)), ((), ()))


def _dot(a, b, dims=NN):
    return lax.dot_general(a, b, dims, preferred_element_type=F32)


def _dot_f32(a, b, dims=NN):
    return lax.dot_general(a, b, dims, preferred_element_type=F32,
                           precision=lax.Precision.HIGHEST)


def _rms(x, g):
    return x * lax.rsqrt(jnp.mean(x * x, axis=-1, keepdims=True) + EPS) * g


def _sigmoid(x):
    return 1.0 / (1.0 + jnp.exp(-x))


def _log_sigmoid(x):
    return jnp.minimum(x, 0.0) - jnp.log(1.0 + jnp.exp(-jnp.abs(x)))


def _params(*sem):
    return pltpu.CompilerParams(dimension_semantics=sem, vmem_limit_bytes=VMEM_LIMIT)


def _resident(shape):
    n = len(shape)
    return pl.BlockSpec(shape, lambda *_: (0,) * n, pipeline_mode=pl.Buffered(1))


def _norm_mm_kernel(x_ref, g_ref, w_ref, b_ref, o_ref):
    h = _rms(x_ref[...], g_ref[...]).astype(BF16)
    o_ref[...] = (_dot(h, w_ref[...]) + b_ref[...]).astype(o_ref.dtype)


def norm_matmul(x2d, g, w, b, *, tm, out_dtype=F32):
    t, d = x2d.shape
    n = w.shape[1]
    return pl.pallas_call(
        _norm_mm_kernel,
        grid=(t // tm,),
        in_specs=[pl.BlockSpec((tm, d), lambda i: (i, 0)), _resident((1, d)),
                  _resident((d, n)), _resident((1, n))],
        out_specs=pl.BlockSpec((tm, n), lambda i: (i, 0)),
        out_shape=jax.ShapeDtypeStruct((t, n), out_dtype),
        compiler_params=_params("parallel"),
        name="norm_matmul",
    )(x2d, g.reshape(1, d), w, b.reshape(1, n))


def _rope_tables(s):
    half = ROPE_DIMS // 2
    inv_freq = jnp.power(ROPE_THETA, -jnp.arange(half, dtype=F32) * 2.0 / ROPE_DIMS)
    ang = jnp.arange(s).astype(F32)[:, None] * inv_freq[None, :]
    cos, sin = jnp.cos(ang), jnp.sin(ang)
    ones = jnp.ones((s, MOBA_HEAD_DIM - ROPE_DIMS), F32)
    zeros = jnp.zeros((s, MOBA_HEAD_DIM - ROPE_DIMS), F32)
    zh = jnp.zeros((s, half), F32)
    c = jnp.concatenate([cos, cos, ones], axis=1)
    sa = jnp.concatenate([zh, sin, zeros], axis=1)
    sb = jnp.concatenate([-sin, zh, zeros], axis=1)
    rep = LANES // MOBA_HEAD_DIM
    return tuple(jnp.tile(a, (1, rep)) for a in (c, sa, sb))


def _moba_proj_kernel(x_ref, g_ref, w_ref, b_ref, c_ref, sa_ref, sb_ref,
                      q_ref, k_ref, vt_ref, km_ref):
    h = _rms(x_ref[...], g_ref[...]).astype(BF16)
    z = _dot(h, w_ref[...]) + b_ref[...]
    tm = z.shape[0]
    rep = MIX // LANES
    c = jnp.concatenate([c_ref[...]] * rep, axis=1)
    sa = jnp.concatenate([sa_ref[...]] * rep, axis=1)
    sb = jnp.concatenate([sb_ref[...]] * rep, axis=1)
    half = ROPE_DIMS // 2

    def rot(t):
        return t * c + pltpu.roll(t, half, 1) * sa + pltpu.roll(t, MIX - half, 1) * sb

    q = rot(z[:, :MIX]) * (MOBA_HEAD_DIM ** -0.5 * LOG2E)
    k = rot(z[:, MIX:2 * MIX])
    q_ref[...] = q.astype(BF16)
    k_ref[...] = k.astype(BF16)
    nb = tm // MOBA_BLOCK
    for i in range(nb):
        rows = slice(i * MOBA_BLOCK, (i + 1) * MOBA_BLOCK)
        vt_ref[i] = z[rows, 2 * MIX:].T.astype(BF16)
        km_ref[i] = jnp.mean(k[rows, :], axis=0, keepdims=True)


def moba_proj(x2d, g, w, b, tables, s, *, tm):
    t, d = x2d.shape
    n = w.shape[1]
    c, sa, sb = tables
    spt = s // tm
    tab = pl.BlockSpec((tm, LANES), lambda i: (i % spt, 0))
    row = pl.BlockSpec((tm, MIX), lambda i: (i, 0))
    nb = tm // MOBA_BLOCK
    return pl.pallas_call(
        _moba_proj_kernel,
        grid=(t // tm,),
        in_specs=[pl.BlockSpec((tm, d), lambda i: (i, 0)), _resident((1, d)),
                  _resident((d, n)), _resident((1, n)), tab, tab, tab],
        out_specs=[row, row, pl.BlockSpec((nb, MIX, MOBA_BLOCK), lambda i: (i, 0, 0)),
                   pl.BlockSpec((nb, 1, MIX), lambda i: (i, 0, 0))],
        out_shape=[jax.ShapeDtypeStruct((t, MIX), BF16)] * 2
        + [jax.ShapeDtypeStruct((t // MOBA_BLOCK, MIX, MOBA_BLOCK), BF16),
           jax.ShapeDtypeStruct((t // MOBA_BLOCK, 1, MIX), F32)],
        compiler_params=_params("parallel"),
        name="moba_proj",
    )(x2d, g.reshape(1, d), w, b.reshape(1, n), c, sa, sb)


MOBA_TQ = MOBA_BLOCK
HEADS_PER_STEP = LANES // MOBA_HEAD_DIM
SUM_ROWS = 16


def _moba_attn_kernel(q_ref, k_ref, vt_ref, km_ref, o_ref, bias_ref, qh_ref, s_ref, p_ref):
    cur = pl.program_id(2)
    nb = km_ref.shape[1]
    hd = MOBA_HEAD_DIM
    neg = -jnp.inf
    own = pl.multiple_of(cur * MOBA_BLOCK, MOBA_BLOCK)
    nh = HEADS_PER_STEP

    ones_rows = jnp.ones((SUM_ROWS, MOBA_BLOCK), BF16)

    def v_rows(blk, hh):
        return jnp.concatenate([vt_ref[0, blk, hh * hd:(hh + 1) * hd, :], ones_rows], axis=0)

    def scores_into(slot, blk):
        off = pl.multiple_of(blk * MOBA_BLOCK, MOBA_BLOCK)
        kb = k_ref[0, pl.ds(off, MOBA_BLOCK), :]
        for hh in range(nh):
            s_ref[slot, hh] = _dot(kb, qh_ref[hh], NT)

    q_all = q_ref[0]
    lane = lax.broadcasted_iota(I32, q_all.shape, 1)
    km = km_ref[0]
    km_hi = km.astype(BF16)
    km_lo = (km - km_hi.astype(F32)).astype(BF16)
    k_own = k_ref[0, pl.ds(own, MOBA_BLOCK), :]
    init = []
    for hh in range(nh):
        q = jnp.where((lane >= hh * hd) & (lane < (hh + 1) * hd), q_all, jnp.zeros_like(q_all))
        qh_ref[hh] = q
        gate = _dot(km_hi, q, NT) + _dot(km_lo, q, NT)
        blk = lax.broadcasted_iota(I32, gate.shape, 0)
        gate = jnp.where(blk < cur, gate, neg)
        sel = jnp.zeros(gate.shape, F32)
        for _ in range(MOBA_TOPK):
            m = jnp.max(gate, axis=0, keepdims=True)
            first = jnp.min(jnp.where(gate == m, blk, nb), axis=0, keepdims=True)
            first = jnp.where(m > neg, first, nb)
            pick = blk == first
            sel = jnp.where(pick, 1.0, sel)
            gate = jnp.where(pick, neg, gate)
        bias_ref[hh] = jnp.where(sel > 0.0, 0.0, neg)

        s = _dot(k_own, q, NT)
        k_pos = lax.broadcasted_iota(I32, s.shape, 0)
        q_pos = lax.broadcasted_iota(I32, s.shape, 1)
        s = jnp.where(k_pos <= q_pos, s, neg)
        m0 = jnp.max(s, axis=0, keepdims=True)
        p = jnp.exp2(s - m0)
        p_ref[hh] = p.astype(BF16)
        init.append((m0, jnp.zeros((hd + SUM_ROWS, MOBA_TQ), F32), jnp.ones((1, MOBA_TQ), F32)))
    scores_into(0, 0)

    def step(j, slot, carry):
        pend = jnp.where(j == 0, cur, j - 1)
        pv = [_dot(v_rows(pend, hh), p_ref[hh]) for hh in range(nh)]
        out = []
        for hh in range(nh):
            m_i, acc, a_pend = carry[hh]
            s = s_ref[slot, hh] + bias_ref[hh, pl.ds(j, 1), :]
            m_new = jnp.maximum(m_i, jnp.max(s, axis=0, keepdims=True))
            alpha = jnp.exp2(m_i - m_new)
            acc_new = a_pend * acc + pv[hh]
            p_ref[hh] = jnp.exp2(s - m_new).astype(BF16)
            out.append((m_new, acc_new, alpha))
        scores_into(1 - slot, jnp.minimum(j + 1, nb - 1))
        return tuple(out)

    def body(i, carry):
        return step(2 * i + 1, 1, step(2 * i, 0, carry))

    trips = (cur + 1) // 2
    final = lax.fori_loop(0, trips, body, tuple(init))
    pend = jnp.where(trips == 0, cur, 2 * trips - 1)
    outs = []
    for hh in range(nh):
        _, acc, a_pend = final[hh]
        acc = a_pend * acc + _dot(v_rows(pend, hh), p_ref[hh])
        outs.append((acc[:hd] / acc[hd:hd + 1]).T)
    o_ref[0] = jnp.concatenate(outs, axis=1).astype(o_ref.dtype)


def moba_attention(q, k, vt, km):
    b, s, _ = q.shape
    nb = km.shape[1]
    qspec = pl.BlockSpec((1, MOBA_TQ, LANES), lambda bi, hp, qi: (bi, qi, hp))
    return pl.pallas_call(
        _moba_attn_kernel,
        grid=(b, MIX // LANES, s // MOBA_TQ),
        in_specs=[qspec,
                  pl.BlockSpec((1, s, LANES), lambda bi, hp, qi: (bi, 0, hp)),
                  pl.BlockSpec((1, nb, LANES, MOBA_BLOCK), lambda bi, hp, qi: (bi, 0, hp, 0)),
                  pl.BlockSpec((1, nb, LANES), lambda bi, hp, qi: (bi, 0, hp))],
        out_specs=qspec,
        out_shape=jax.ShapeDtypeStruct((b, s, MIX), BF16),
        scratch_shapes=[pltpu.VMEM((HEADS_PER_STEP, nb, MOBA_TQ), F32),
                        pltpu.VMEM((HEADS_PER_STEP, MOBA_TQ, LANES), BF16),
                        pltpu.VMEM((2, HEADS_PER_STEP, MOBA_BLOCK, MOBA_TQ), F32),
                        pltpu.VMEM((HEADS_PER_STEP, MOBA_BLOCK, MOBA_TQ), BF16)],
        compiler_params=_params("parallel", "parallel", "arbitrary"),
        name="moba_attention",
    )(q, k, vt, km)


SCAN_ROWS = 512


SUB = 128


def _split3(x):
    hi = x.astype(BF16)
    r1 = x - hi.astype(F32)
    mid = r1.astype(BF16)
    lo = (r1 - mid.astype(F32)).astype(BF16)
    return hi, mid, lo


def _dot_exact_lhs(lhs_bf16, x):
    return sum(_dot(lhs_bf16, t) for t in _split3(x))


def _chunk_masks(n):
    ri = lax.broadcasted_iota(I32, (n, n), 0)
    ci = lax.broadcasted_iota(I32, (n, n), 1)
    same = (ri // CHUNK) == (ci // CHUNK)
    return same, same & (ci <= ri)


def _gla_kernel(qk_ref, v_ref, r_ref, a_ref, gw_ref, gb_ref, ng_ref, o_ref, st_ref):
    @pl.when(pl.program_id(1) == 0)
    def _():
        st_ref[...] = jnp.zeros_like(st_ref)

    dk = GLA_DK // GLA_HEADS
    dv = GLA_DV // GLA_HEADS
    L = CHUNK
    same, causal = _chunk_masks(SUB)
    tri = jnp.where(causal, 1.0, 0.0).astype(BF16)
    blk = jnp.where(same, 1.0, 0.0).astype(BF16)
    lane = lax.broadcasted_iota(I32, (SUB, LANES), 1)
    gw = gw_ref[...]
    gb = gb_ref[...]
    ng = ng_ref[...]
    nsub = SCAN_ROWS // SUB
    subs = [slice(i * SUB, (i + 1) * SUB) for i in range(nsub)]
    heads = range(GLA_HEADS)
    chunks = [slice(c * L, (c + 1) * L) for c in range(SUB // L)]
    pair = lambda h: slice((h // 2) * LANES, (h // 2 + 1) * LANES)

    log_a = [_log_sigmoid(_dot_f32(a_ref[rows, :], gw) + gb) / GLA_TAU for rows in subs]
    cum = [_dot_exact_lhs(tri, x) for x in log_a]
    tot = [_dot_exact_lhs(blk, x) for x in log_a]
    q_dec, k_inv, k_end, dec, v = [], [], [], [], []
    for i, rows in enumerate(subs):
        qk = qk_ref[rows, :]
        k = qk[:, GLA_DK:]
        q_dec.append((qk[:, :GLA_DK] * (dk ** -0.5) * jnp.exp(cum[i])).astype(BF16))
        k_inv.append((k * jnp.exp(-cum[i])).astype(BF16))
        k_end.append((k * jnp.exp(tot[i] - cum[i])).astype(BF16))
        dec.append(jnp.exp(tot[i]))
        v.append(v_ref[rows, :].astype(BF16))

    def q_masked(i, h):
        lo_lane = (h % 2) * dk
        return jnp.where((lane >= lo_lane) & (lane < lo_lane + dk), q_dec[i][:, pair(h)],
                         jnp.zeros((SUB, LANES), BF16))

    qm = [[q_masked(i, h) for h in heads] for i in range(nsub)]
    vh = [[v[i][:, h * dv:(h + 1) * dv] for h in heads] for i in range(nsub)]
    att = [[_dot(qm[i][h], k_inv[i][:, pair(h)], NT) for h in heads] for i in range(nsub)]
    kv_t = [[[_dot(vh[i][h][cr], k_end[i][cr, pair(h)], TN) for cr in chunks] for h in heads]
            for i in range(nsub)]
    intra = [[_dot(jnp.where(causal, att[i][h], 0.0).astype(BF16), vh[i][h]) for h in heads]
             for i in range(nsub)]

    prev = [[[None] * len(chunks) for _ in heads] for _ in range(nsub)]
    for h in heads:
        state = st_ref[h]
        for i in range(nsub):
            for c in range(len(chunks)):
                prev[i][h][c] = state.astype(BF16)
                state = state * dec[i][c * L:c * L + 1, pair(h)] + kv_t[i][h][c]
        st_ref[h] = state

    for i, rows in enumerate(subs):
        outs = []
        for h in heads:
            inter = [_dot(qm[i][h][cr], prev[i][h][c], NT) for c, cr in enumerate(chunks)]
            o = intra[i][h] + jnp.concatenate(inter, axis=0)
            outs.append(o * lax.rsqrt(jnp.mean(o * o, axis=-1, keepdims=True) + EPS))
        r = r_ref[rows, :]
        o_ref[rows, :] = (jnp.concatenate(outs, axis=1) * ng * (r * _sigmoid(r))).astype(o_ref.dtype)


def gla_scan(z, gate_w_pad, gate_b, norm_g, b, s):
    t = z.shape[0]
    nc = s // SCAN_ROWS
    wide = lambda col: pl.BlockSpec((SCAN_ROWS, MIX), lambda bi, c: (bi * nc + c, col))
    dk = GLA_DK // GLA_HEADS
    dv = GLA_DV // GLA_HEADS
    return pl.pallas_call(
        _gla_kernel,
        grid=(b, nc),
        in_specs=[wide(0), wide(1), wide(2),
                  pl.BlockSpec((SCAN_ROWS, LANES), lambda bi, c: (bi * nc + c, 3 * MIX // LANES)),
                  _resident((LANES, GLA_DK)), _resident((1, GLA_DK)), _resident((1, GLA_DV))],
        out_specs=pl.BlockSpec((SCAN_ROWS, MIX), lambda bi, c: (bi * nc + c, 0)),
        out_shape=jax.ShapeDtypeStruct((t, MIX), BF16),
        scratch_shapes=[pltpu.VMEM((GLA_HEADS, dv, LANES), F32)],
        compiler_params=_params("parallel", "arbitrary"),
        name="gla_scan",
    )(z, z, z, z, gate_w_pad, gate_b.reshape(1, GLA_DK), norm_g.reshape(1, GLA_DV))


TAIL = 8


def _mlstm_kernel(qk_ref, v_ref, og_ref, ifc_ref, ifr_ref, cw_ref, cb_ref, o_ref,
                  tail_ref, ct_ref, n_ref, m_ref, qkc_ref):
    @pl.when(pl.program_id(1) == 0)
    def _():
        tail_ref[...] = jnp.zeros_like(tail_ref)
        ct_ref[...] = jnp.zeros_like(ct_ref)
        n_ref[...] = jnp.zeros_like(n_ref)
        m_ref[...] = jnp.zeros_like(m_ref)

    nh = MLSTM_HEADS
    dk = MLSTM_DQK // nh
    dv = MLSTM_DV // nh
    L = CHUNK
    rows_n = qk_ref.shape[0]
    neg = -jnp.inf

    x = qk_ref[...]
    xx = jnp.concatenate([tail_ref[...], x], axis=0)
    cw = cw_ref[...]
    y = cb_ref[...] + sum(
        cw[w:w + 1, :] * xx[TAIL - (MLSTM_CONV - 1) + w:TAIL - (MLSTM_CONV - 1) + w + rows_n, :]
        for w in range(MLSTM_CONV))
    tail_ref[...] = x[rows_n - TAIL:, :]
    qkc_ref[...] = y * _sigmoid(y)

    same, causal = _chunk_masks(SUB)
    tri = jnp.where(causal, 1.0, 0.0).astype(BF16)
    blk = jnp.where(same, 1.0, 0.0).astype(BF16)
    ri = lax.broadcasted_iota(I32, (SUB, SUB), 0)
    ci = lax.broadcasted_iota(I32, (SUB, SUB), 1)
    tri_t = jnp.where(same & (ri <= ci), 1.0, 0.0).astype(BF16)

    nsub = rows_n // SUB
    subs = [slice(i * SUB, (i + 1) * SUB) for i in range(nsub)]
    heads = range(nh)
    chunks = [slice(c * L, (c + 1) * L) for c in range(SUB // L)]
    nck = len(chunks)
    hcol = lambda h: slice(h * dk, (h + 1) * dk)

    ifc = [ifc_ref[rows, :] for rows in subs]
    ifr = [ifr_ref[i] for i in range(nsub)]
    lf_c = [_log_sigmoid(t) for t in ifc]
    cum_c = [_dot_exact_lhs(tri, t) for t in lf_c]
    tot_c = [_dot_exact_lhs(blk, t) for t in lf_c]
    cum_r = [sum(_dot(t, tri_t) for t in _split3(_log_sigmoid(u))) for u in ifr]

    a_end = [[tot_c[i][:, nh + h:nh + h + 1] - cum_c[i][:, nh + h:nh + h + 1] + ifc[i][:, h:h + 1]
              for h in heads] for i in range(nsub)]
    m_prev = [[[None] * nck for _ in heads] for _ in range(nsub)]
    m_next = [[[None] * nck for _ in heads] for _ in range(nsub)]
    for h in heads:
        m_run = m_ref[h:h + 1, 0:1]
        for i in range(nsub):
            for c, cr in enumerate(chunks):
                tot = tot_c[i][c * L:c * L + 1, nh + h:nh + h + 1]
                m_prev[i][h][c] = m_run
                m_run = jnp.maximum(tot + m_run, jnp.max(a_end[i][h][cr], axis=0, keepdims=True))
                m_next[i][h][c] = m_run
        m_ref[h:h + 1, :] = jnp.broadcast_to(m_run, (1, LANES))

    def per_row(vals):
        return jnp.concatenate([jnp.broadcast_to(t, (L, 1)) for t in vals], axis=0)

    qf = [[qkc_ref[rows, hcol(h)] for h in heads] for rows in subs]
    kf = [[qkc_ref[rows, MLSTM_DQK + h * dk:MLSTM_DQK + (h + 1) * dk] * (dk ** -0.5) for h in heads]
          for rows in subs]
    q = [[t.astype(BF16) for t in r] for r in qf]
    k = [[t.astype(BF16) for t in r] for r in kf]
    v = [[v_ref[rows, h * dv:(h + 1) * dv].astype(BF16) for h in heads] for rows in subs]
    qk = [[_dot(q[i][h], k[i][h], NT) for h in heads] for i in range(nsub)]

    kv_t = [[[None] * nck for _ in heads] for _ in range(nsub)]
    n_add = [[[None] * nck for _ in heads] for _ in range(nsub)]
    dec = [[[None] * nck for _ in heads] for _ in range(nsub)]
    for i in range(nsub):
        for h in heads:
            w = jnp.exp(a_end[i][h] - per_row(m_next[i][h]))
            wk = w * kf[i][h]
            for c, cr in enumerate(chunks):
                tot = tot_c[i][c * L:c * L + 1, nh + h:nh + h + 1]
                dec[i][h][c] = jnp.exp(tot + m_prev[i][h][c] - m_next[i][h][c])
                kv_t[i][h][c] = _dot(v[i][h][cr], wk[cr].astype(BF16), TN)
                n_add[i][h][c] = jnp.sum(wk[cr], axis=0, keepdims=True)

    w_inter = [[None] * nh for _ in range(nsub)]
    m_row = [[None] * nh for _ in range(nsub)]
    w_intra = [[None] * nh for _ in range(nsub)]
    for i in range(nsub):
        for h in heads:
            cum_i = cum_c[i][:, nh + h:nh + h + 1]
            log_d = jnp.where(causal, cum_i - cum_r[i][nh + h:nh + h + 1, :] + ifr[i][h:h + 1, :], neg)
            log_inter = cum_i + per_row(m_prev[i][h])
            m_i = jnp.maximum(log_inter, jnp.max(log_d, axis=1, keepdims=True))
            w_intra[i][h] = qk[i][h] * jnp.exp(log_d - m_i)
            w_inter[i][h] = jnp.exp(log_inter - m_i)
            m_row[i][h] = m_i
    num_intra = [[_dot(w_intra[i][h].astype(BF16), v[i][h]) for h in heads] for i in range(nsub)]

    c_prev = [[[None] * nck for _ in heads] for _ in range(nsub)]
    n_prev = [[[None] * nck for _ in heads] for _ in range(nsub)]
    for h in heads:
        ct = ct_ref[h]
        nv = n_ref[h:h + 1, :]
        for i in range(nsub):
            for c in range(nck):
                c_prev[i][h][c] = ct.astype(BF16)
                n_prev[i][h][c] = nv
                ct = dec[i][h][c] * ct + kv_t[i][h][c]
                nv = dec[i][h][c] * nv + n_add[i][h][c]
        ct_ref[h] = ct
        n_ref[h:h + 1, :] = nv

    for i, rows in enumerate(subs):
        outs = []
        for h in heads:
            inter = jnp.concatenate([_dot(q[i][h][cr], c_prev[i][h][c], NT)
                                     for c, cr in enumerate(chunks)], axis=0)
            qn = jnp.concatenate([jnp.sum(qf[i][h][cr] * n_prev[i][h][c], axis=1, keepdims=True)
                                  for c, cr in enumerate(chunks)], axis=0)
            num = w_inter[i][h] * inter + num_intra[i][h]
            den = w_inter[i][h] * qn + jnp.sum(w_intra[i][h], axis=1, keepdims=True)
            outs.append(num / jnp.maximum(jnp.abs(den), jnp.exp(-m_row[i][h])))
        og = og_ref[rows, :]
        o_ref[rows, :] = (jnp.concatenate(outs, axis=1) * _sigmoid(og)).astype(o_ref.dtype)


def mlstm_scan(z, ifr, conv_w, conv_b, b, s):
    t = z.shape[0]
    nc = s // SCAN_ROWS
    cpt = SCAN_ROWS // SUB
    dk = MLSTM_DQK // MLSTM_HEADS
    dv = MLSTM_DV // MLSTM_HEADS
    row = lambda w, col: pl.BlockSpec((SCAN_ROWS, w), lambda bi, c: (bi * nc + c, col))
    return pl.pallas_call(
        _mlstm_kernel,
        grid=(b, nc),
        in_specs=[row(2 * MLSTM_DQK, 0), row(MIX, 2), row(MIX, 3),
                  row(LANES, 4 * MIX // LANES),
                  pl.BlockSpec((cpt, 2 * MLSTM_HEADS, SUB), lambda bi, c: (bi * nc + c, 0, 0)),
                  _resident((MLSTM_CONV, 2 * MLSTM_DQK)), _resident((1, 2 * MLSTM_DQK))],
        out_specs=row(MIX, 0),
        out_shape=jax.ShapeDtypeStruct((t, MIX), BF16),
        scratch_shapes=[pltpu.VMEM((TAIL, 2 * MLSTM_DQK), F32),
                        pltpu.VMEM((MLSTM_HEADS, dv, dk), F32),
                        pltpu.VMEM((8, dk), F32),
                        pltpu.VMEM((8, LANES), F32),
                        pltpu.VMEM((SCAN_ROWS, 2 * MLSTM_DQK), F32)],
        compiler_params=_params("parallel", "arbitrary"),
        name="mlstm_scan",
    )(z, z, z, z, ifr, conv_w, conv_b.reshape(1, -1))


def _merge_kernel(x_ref, g_ref, wg_ref, bg_ref, ya_ref, yb_ref, yc_ref,
                  wa_ref, wb_ref, wc_ref, wo_ref, o_ref):
    x = x_ref[...]
    h = _rms(x, g_ref[...]).astype(BF16)
    d = D_MODEL
    merged = None
    for i, (y_ref, w_ref) in enumerate(((ya_ref, wa_ref), (yb_ref, wb_ref), (yc_ref, wc_ref))):
        gate = _sigmoid(_dot(h, wg_ref[:, i * d:(i + 1) * d]) + bg_ref[:, i * d:(i + 1) * d])
        term = gate * _dot(y_ref[...], w_ref[...])
        merged = term if merged is None else merged + term
    o_ref[...] = x + _dot(merged.astype(BF16), wo_ref[...])


def merge_out(x2d, g, w_gate, b_gate, ya, yb, yc, wa, wb, wc, wo, *, tm):
    t, d = x2d.shape
    row = lambda w: pl.BlockSpec((tm, w), lambda i: (i, 0))
    return pl.pallas_call(
        _merge_kernel,
        grid=(t // tm,),
        in_specs=[row(d), _resident((1, d)), _resident((d, N_BRANCHES * d)),
                  _resident((1, N_BRANCHES * d)), row(MIX), row(MIX), row(MIX),
                  _resident((MIX, d)), _resident((MIX, d)), _resident((MIX, d)),
                  _resident((d, d))],
        out_specs=row(d),
        out_shape=jax.ShapeDtypeStruct((t, d), F32),
        compiler_params=_params("parallel"),
        name="merge_out",
    )(x2d, g.reshape(1, d), w_gate, b_gate.reshape(1, -1), ya, yb, yc, wa, wb, wc, wo)


def _ffn_kernel(x_ref, g_ref, wg_ref, wu_ref, wd_ref, o_ref, h_ref, acc_ref):
    f = pl.program_id(1)

    @pl.when(f == 0)
    def _():
        h_ref[...] = _rms(x_ref[...], g_ref[...]).astype(BF16)
        acc_ref[...] = jnp.zeros_like(acc_ref)

    h = h_ref[...]
    gt = _dot(h, wg_ref[...])
    up = _dot(h, wu_ref[...])
    act = (gt * _sigmoid(gt) * up).astype(BF16)
    acc_ref[...] += _dot(act, wd_ref[...])

    @pl.when(f == pl.num_programs(1) - 1)
    def _():
        o_ref[...] = x_ref[...] + acc_ref[...]


def _ff_tile(ff):
    for cand in (1408, 1024, 768, 512, 256, 128):
        if ff % cand == 0:
            return cand
    raise ValueError(f"unsupported d_ff {ff}")


def dense_ffn(x2d, g, wg, wu, wd, *, tm):
    t, d = x2d.shape
    ff = wg.shape[1]
    tf = _ff_tile(ff)
    return pl.pallas_call(
        _ffn_kernel,
        grid=(t // tm, ff // tf),
        in_specs=[pl.BlockSpec((tm, d), lambda i, f: (i, 0)), _resident((1, d)),
                  pl.BlockSpec((d, tf), lambda i, f: (0, f)),
                  pl.BlockSpec((d, tf), lambda i, f: (0, f)),
                  pl.BlockSpec((tf, d), lambda i, f: (f, 0))],
        out_specs=pl.BlockSpec((tm, d), lambda i, f: (i, 0)),
        out_shape=jax.ShapeDtypeStruct((t, d), F32),
        scratch_shapes=[pltpu.VMEM((tm, d), BF16), pltpu.VMEM((tm, d), F32)],
        compiler_params=_params("parallel", "arbitrary"),
        name="dense_ffn",
    )(x2d, g.reshape(1, d), wg, wu, wd)


ROUTER_TM = 512
MOE_ROWS = 512
META_E, META_G, META_RANK = 0, 2, 4


def _router_kernel(x_ref, g_ref, rw_ref, rb_ref, h_ref, meta_ref, cnt_ref, carry_ref):
    @pl.when(pl.program_id(0) == 0)
    def _():
        carry_ref[...] = jnp.zeros_like(carry_ref)

    h = _rms(x_ref[...], g_ref[...])
    h_ref[...] = h
    tm = h.shape[0]
    lane = lax.broadcasted_iota(I32, (tm, LANES), 1)
    neg = -jnp.inf
    logits = jnp.where(lane < N_EXPERTS, _dot_f32(h, rw_ref[...]) + rb_ref[...], neg)
    m1 = jnp.max(logits, axis=1, keepdims=True)
    e1 = jnp.min(jnp.where(logits == m1, lane, LANES), axis=1, keepdims=True)
    rest = jnp.where(lane == e1, neg, logits)
    m2 = jnp.max(rest, axis=1, keepdims=True)
    e2 = jnp.min(jnp.where(rest == m2, lane, LANES), axis=1, keepdims=True)
    ex = jnp.exp(m2 - m1)
    g1 = 1.0 / (1.0 + ex)
    g2 = ex / (1.0 + ex)
    oh1 = jnp.where(lane == e1, 1.0, 0.0)
    oh2 = jnp.where(lane == e2, 1.0, 0.0)
    cnt = oh1 + oh2
    ri = lax.broadcasted_iota(I32, (tm, tm), 0)
    ci = lax.broadcasted_iota(I32, (tm, tm), 1)
    strict = jnp.where(ci < ri, 1.0, 0.0).astype(BF16)
    excl = _dot(strict, cnt.astype(BF16)) + carry_ref[...]
    r1 = jnp.sum(excl * oh1, axis=1, keepdims=True)
    r2 = jnp.sum(excl * oh2, axis=1, keepdims=True)
    carry_ref[...] += jnp.sum(cnt, axis=0, keepdims=True)
    cnt_ref[...] = carry_ref[...]
    meta = jnp.zeros((tm, LANES), F32)
    for pos, val in ((META_E, e1.astype(F32)), (META_E + 1, e2.astype(F32)),
                     (META_G, g1), (META_G + 1, g2), (META_RANK, r1), (META_RANK + 1, r2)):
        meta = jnp.where(lane == pos, val, meta)
    meta_ref[...] = meta


def moe_router(x2d, g, rw_pad, rb_pad):
    t, d = x2d.shape
    tm = ROUTER_TM
    return pl.pallas_call(
        _router_kernel,
        grid=(t // tm,),
        in_specs=[pl.BlockSpec((tm, d), lambda i: (i, 0)), _resident((1, d)),
                  _resident((d, LANES)), _resident((1, LANES))],
        out_specs=[pl.BlockSpec((tm, d), lambda i: (i, 0)),
                   pl.BlockSpec((tm, LANES), lambda i: (i, 0)),
                   pl.BlockSpec((1, LANES), lambda i: (0, 0))],
        out_shape=[jax.ShapeDtypeStruct((t, d), F32), jax.ShapeDtypeStruct((t, LANES), F32),
                   jax.ShapeDtypeStruct((1, LANES), F32)],
        scratch_shapes=[pltpu.VMEM((1, LANES), F32)],
        compiler_params=_params("arbitrary"),
        name="moe_router",
    )(x2d, g.reshape(1, d), rw_pad, rb_pad)


GATHER_ROWS = 2048


def _row_copy(src_hbm, dst_ref, src_row, dst_row, sem):
    return pltpu.make_async_copy(src_hbm.at[pl.ds(src_row, 1)], dst_ref.at[pl.ds(dst_row, 1)], sem)


def _gather_kernel(idx_ref, src_hbm, o_ref, sem):
    n = o_ref.shape[0]

    def start(r, _):
        _row_copy(src_hbm, o_ref, idx_ref[0, 0, r], r, sem).start()
        return 0

    def wait(r, _):
        _row_copy(src_hbm, o_ref, 0, r, sem).wait()
        return 0

    lax.fori_loop(0, n, start, 0)
    lax.fori_loop(0, n, wait, 0)


def gather_rows(src, idx):
    n = idx.shape[0]
    d = src.shape[1]
    nblk = n // GATHER_ROWS
    return pl.pallas_call(
        _gather_kernel,
        grid=(nblk,),
        in_specs=[pl.BlockSpec((1, 1, GATHER_ROWS), lambda i: (i, 0, 0), memory_space=pltpu.SMEM),
                  pl.BlockSpec(memory_space=pl.ANY)],
        out_specs=pl.BlockSpec((GATHER_ROWS, d), lambda i: (i, 0)),
        out_shape=jax.ShapeDtypeStruct((n, d), src.dtype),
        scratch_shapes=[pltpu.SemaphoreType.DMA(())],
        compiler_params=_params("arbitrary"),
        name="gather_rows",
    )(idx.reshape(nblk, 1, GATHER_ROWS), src)


def _grouped_ffn_kernel(be_ref, bv_ref, x_ref, wg_ref, wu_ref, wd_ref, o_ref, acc_ref):
    i = pl.program_id(0)
    f = pl.program_id(1)

    @pl.when(f == 0)
    def _():
        acc_ref[...] = jnp.zeros_like(acc_ref)

    @pl.when(bv_ref[i] > 0)
    def _():
        h = x_ref[...].astype(BF16)
        gt = _dot(h, wg_ref[...])
        up = _dot(h, wu_ref[...])
        act = (gt * _sigmoid(gt) * up).astype(BF16)
        acc_ref[...] += _dot(act, wd_ref[...])

    @pl.when(f == pl.num_programs(1) - 1)
    def _():
        o_ref[...] = acc_ref[...]


def grouped_ffn(xs, blk_e, blk_valid, wg, wu, wd):
    n, d = xs.shape
    ff = wg.shape[2]
    tf = _ff_tile(ff)
    nblk = n // MOE_ROWS
    grid_spec = pltpu.PrefetchScalarGridSpec(
        num_scalar_prefetch=2,
        grid=(nblk, ff // tf),
        in_specs=[pl.BlockSpec((MOE_ROWS, d), lambda i, f, be, bv: (i, 0)),
                  pl.BlockSpec((None, d, tf), lambda i, f, be, bv: (be[i], 0, f)),
                  pl.BlockSpec((None, d, tf), lambda i, f, be, bv: (be[i], 0, f)),
                  pl.BlockSpec((None, tf, d), lambda i, f, be, bv: (be[i], f, 0))],
        out_specs=pl.BlockSpec((MOE_ROWS, d), lambda i, f, be, bv: (i, 0)),
        scratch_shapes=[pltpu.VMEM((MOE_ROWS, d), F32)],
    )
    return pl.pallas_call(
        _grouped_ffn_kernel,
        grid_spec=grid_spec,
        out_shape=jax.ShapeDtypeStruct((n, d), F32),
        compiler_params=_params("parallel", "arbitrary"),
        name="grouped_ffn",
    )(blk_e, blk_valid, xs, wg, wu, wd)


COMBINE_TM = 1024


def _combine_kernel(d1_ref, d2_ref, x_ref, meta_ref, fg_ref, ys_hbm, o_ref, b1_ref, b2_ref, sem):
    n = x_ref.shape[0]

    def start(r, _):
        _row_copy(ys_hbm, b1_ref, d1_ref[0, 0, r], r, sem.at[0]).start()
        _row_copy(ys_hbm, b2_ref, d2_ref[0, 0, r], r, sem.at[1]).start()
        return 0

    def wait(r, _):
        _row_copy(ys_hbm, b1_ref, 0, r, sem.at[0]).wait()
        _row_copy(ys_hbm, b2_ref, 0, r, sem.at[1]).wait()
        return 0

    lax.fori_loop(0, n, start, 0)
    lax.fori_loop(0, n, wait, 0)
    meta = meta_ref[...]
    g1 = meta[:, META_G:META_G + 1]
    g2 = meta[:, META_G + 1:META_G + 2]
    y = x_ref[...] + (g1 * b1_ref[...] + g2 * b2_ref[...])
    o_ref[...] = _rms(y, fg_ref[...])


def moe_combine_norm(x2d, meta, dest1, dest2, ys, final_g):
    t, d = x2d.shape
    tm = COMBINE_TM
    nblk = t // tm
    idx = pl.BlockSpec((1, 1, tm), lambda i: (i, 0, 0), memory_space=pltpu.SMEM)
    return pl.pallas_call(
        _combine_kernel,
        grid=(nblk,),
        in_specs=[idx, idx, pl.BlockSpec((tm, d), lambda i: (i, 0)),
                  pl.BlockSpec((tm, LANES), lambda i: (i, 0)), _resident((1, d)),
                  pl.BlockSpec(memory_space=pl.ANY)],
        out_specs=pl.BlockSpec((tm, d), lambda i: (i, 0)),
        out_shape=jax.ShapeDtypeStruct((t, d), F32),
        scratch_shapes=[pltpu.VMEM((tm, d), F32), pltpu.VMEM((tm, d), F32),
                        pltpu.SemaphoreType.DMA((2,))],
        compiler_params=_params("arbitrary"),
        name="moe_combine_norm",
    )(dest1.reshape(nblk, 1, tm), dest2.reshape(nblk, 1, tm), x2d, meta,
      final_g.reshape(1, d), ys)


def moe_layer_final(x2d, norm_g, router_w, router_b, wg, wu, wd, final_g):
    t, d = x2d.shape
    rw_pad = jnp.pad(router_w, ((0, 0), (0, LANES - N_EXPERTS)))
    rb_pad = jnp.pad(router_b, (0, LANES - N_EXPERTS)).reshape(1, LANES)
    h, meta, counts = moe_router(x2d, norm_g, rw_pad, rb_pad)

    counts = counts[0, :N_EXPERTS].astype(I32)
    padded = (counts + MOE_ROWS - 1) // MOE_ROWS * MOE_ROWS
    pad_ends = jnp.cumsum(padded)
    pad_starts = pad_ends - padded
    e = meta[:, META_E:META_E + TOP_K].astype(I32)
    rank = meta[:, META_RANK:META_RANK + TOP_K].astype(I32)
    dest = pad_starts[e] + rank
    n_blk = (t * TOP_K) // MOE_ROWS + N_EXPERTS
    n_rows = n_blk * MOE_ROWS
    tok = jnp.broadcast_to(jnp.arange(t, dtype=I32)[:, None], (t, TOP_K))
    rows_tok = jnp.zeros((n_rows,), I32).at[dest.reshape(-1)].set(tok.reshape(-1))
    blk_start = jnp.arange(n_blk, dtype=I32) * MOE_ROWS
    blk_valid = (blk_start < pad_ends[-1]).astype(I32)
    blk_e = jnp.minimum(jnp.sum((blk_start[:, None] >= pad_ends[None, :]).astype(I32), axis=1),
                        N_EXPERTS - 1)

    xs = gather_rows(h, rows_tok)
    ys = grouped_ffn(xs, blk_e, blk_valid, wg, wu, wd)
    return moe_combine_norm(x2d, meta, dest[:, 0], dest[:, 1], ys, final_g)


def _col_slices():
    sizes = (('moba', 3 * MIX), ('gla_qkv', 2 * GLA_DK + GLA_DV), ('gla_a', GLA_GATE_RANK),
             ('gla_r', GLA_DV), ('mlstm_qkv', 2 * MLSTM_DQK + MLSTM_DV),
             ('mlstm_if', 2 * MLSTM_HEADS), ('mlstm_o', MLSTM_DV),
             ('merge_gate', N_BRANCHES * D_MODEL))
    out, off = {}, 0
    for name, size in sizes:
        out[name] = slice(off, off + size)
        off += size
    return out


def _pad_cols(a, width):
    return jnp.pad(a, ((0, 0), (0, width - a.shape[1])))


def hybrid_mixer_layer(x2d, b, s, norm_g, w_in, b_in, conv_w, conv_b, gate_w, gate_b, gla_norm_g,
                       w_up_moba, w_up_gla, w_up_mlstm, w_o, tables):
    cs = _col_slices()
    bias = b_in.reshape(1, -1)
    wcat = lambda names: jnp.concatenate([w_in[:, cs[n]] for n in names], axis=1)
    bcat = lambda names: jnp.concatenate([bias[:, cs[n]] for n in names], axis=1)

    nb = s // MOBA_BLOCK
    q, k, vt, km = moba_proj(x2d, norm_g, w_in[:, cs['moba']].astype(BF16), bias[:, cs['moba']],
                             tables, s, tm=512)
    y_moba = moba_attention(q.reshape(b, s, MIX), k.reshape(b, s, MIX),
                            vt.reshape(b, nb, MIX, MOBA_BLOCK),
                            km.reshape(b, nb, MIX)).reshape(b * s, MIX)

    w_gla = jnp.concatenate([wcat(['gla_qkv', 'gla_r']), _pad_cols(w_in[:, cs['gla_a']], LANES)], axis=1)
    b_gla = jnp.concatenate([bcat(['gla_qkv', 'gla_r']), _pad_cols(bias[:, cs['gla_a']], LANES)], axis=1)
    z_gla = norm_matmul(x2d, norm_g, w_gla.astype(BF16), b_gla, tm=512)
    gate_w_pad = jnp.pad(gate_w, ((0, LANES - GLA_GATE_RANK), (0, 0)))
    y_gla = gla_scan(z_gla, gate_w_pad, gate_b, gla_norm_g, b, s)

    w_ml = jnp.concatenate([wcat(['mlstm_qkv', 'mlstm_o']), _pad_cols(w_in[:, cs['mlstm_if']], LANES)], axis=1)
    b_ml = jnp.concatenate([bcat(['mlstm_qkv', 'mlstm_o']), _pad_cols(bias[:, cs['mlstm_if']], LANES)], axis=1)
    z_ml = norm_matmul(x2d, norm_g, w_ml.astype(BF16), b_ml, tm=512)
    n_if = 2 * MLSTM_HEADS
    ifr = z_ml[:, 4 * MIX:4 * MIX + n_if].reshape(b * s // SUB, SUB, n_if).transpose(0, 2, 1)
    y_ml = mlstm_scan(z_ml, ifr, conv_w, conv_b, b, s)

    return merge_out(x2d, norm_g, w_in[:, cs['merge_gate']].astype(BF16), bias[:, cs['merge_gate']],
                     y_moba, y_gla, y_ml, w_up_moba.astype(BF16), w_up_gla.astype(BF16),
                     w_up_mlstm.astype(BF16), w_o.astype(BF16), tm=512)


def kernel(x, attn_norm_g, w_in, b_in, mlstm_conv_w, mlstm_conv_b, gla_gate_w, gla_gate_b, gla_norm_g, w_up_moba, w_up_gla, w_up_mlstm, w_o, ffn_norm_g, ffn_w_gate, ffn_w_up, ffn_w_down, router_w, router_b, moe_w_gate, moe_w_up, moe_w_down, final_norm_g):
    b, s, d = x.shape
    depth = w_in.shape[0]
    assert d == D_MODEL and depth == 2, "two layers: dense SwiGLU then MoE"
    assert s % SCAN_ROWS == 0 and s % MOBA_BLOCK == 0
    x2d = x.reshape(b * s, d)
    tables = _rope_tables(s)
    for l in range(depth):
        x2d = hybrid_mixer_layer(x2d, b, s, attn_norm_g[l], w_in[l], b_in[l], mlstm_conv_w[l],
                                 mlstm_conv_b[l], gla_gate_w[l], gla_gate_b[l], gla_norm_g[l],
                                 w_up_moba[l], w_up_gla[l], w_up_mlstm[l], w_o[l], tables)
        j = l // 2
        if l % 2 == 0:
            x2d = dense_ffn(x2d, ffn_norm_g[l], ffn_w_gate[j].astype(BF16), ffn_w_up[j].astype(BF16),
                            ffn_w_down[j].astype(BF16), tm=512)
        else:
            x2d = moe_layer_final(x2d, ffn_norm_g[l], router_w[j], router_b[j],
                                  moe_w_gate[j].astype(BF16), moe_w_up[j].astype(BF16),
                                  moe_w_down[j].astype(BF16), final_norm_g)
    return x2d.reshape(b, s, d)
```

```python
import functools

import jax
import jax.numpy as jnp
from jax import lax
from jax.experimental import pallas as pl
from jax.experimental.pallas import tpu as pltpu

F32 = jnp.float32
BF16 = jnp.bfloat16
I32 = jnp.int32

D_MODEL = 1024
MIX = D_MODEL // 2
MOBA_HEAD_DIM = 64
MOBA_BLOCK = 256
MOBA_TOPK = 3
ROPE_THETA = 500000.0
ROPE_DIMS = MOBA_HEAD_DIM // 4
GLA_HEADS = 4
GLA_DK = MIX // 2
GLA_DV = MIX
GLA_GATE_RANK = 16
GLA_TAU = 16.0
MLSTM_HEADS = 4
MLSTM_DQK = MIX
MLSTM_DV = MIX
MLSTM_CONV = 4
CHUNK = 64
N_BRANCHES = 3
N_EXPERTS = 8
TOP_K = 2
EPS = 1e-6
LOG2E = 1.4426950408889634

LANES = 128
VMEM_LIMIT = 56 * 1024 * 1024

NN = (((1,), (0,)), ((), ()))
NT = (((1,), (1,)), ((), ()))
TN = (((0,), (0,)), ((), ()))


def _dot(a, b, dims=NN):
    return lax.dot_general(a, b, dims, preferred_element_type=F32)


def _dot_f32(a, b, dims=NN):
    return lax.dot_general(a, b, dims, preferred_element_type=F32,
                           precision=lax.Precision.HIGHEST)


def _rms(x, g):
    return x * lax.rsqrt(jnp.mean(x * x, axis=-1, keepdims=True) + EPS) * g


def _sigmoid(x):
    return 1.0 / (1.0 + jnp.exp(-x))


def _log_sigmoid(x):
    return jnp.minimum(x, 0.0) - jnp.log(1.0 + jnp.exp(-jnp.abs(x)))


def _params(*sem):
    return pltpu.CompilerParams(dimension_semantics=sem, vmem_limit_bytes=VMEM_LIMIT)


def _resident(shape):
    n = len(shape)
    return pl.BlockSpec(shape, lambda *_: (0,) * n, pipeline_mode=pl.Buffered(1))


def _norm_mm_kernel(x_ref, g_ref, w_ref, b_ref, o_ref):
    h = _rms(x_ref[...], g_ref[...]).astype(BF16)
    o_ref[...] = (_dot(h, w_ref[...]) + b_ref[...]).astype(o_ref.dtype)


def norm_matmul(x2d, g, w, b, *, tm, out_dtype=F32):
    t, d = x2d.shape
    n = w.shape[1]
    return pl.pallas_call(
        _norm_mm_kernel,
        grid=(t // tm,),
        in_specs=[pl.BlockSpec((tm, d), lambda i: (i, 0)), _resident((1, d)),
                  _resident((d, n)), _resident((1, n))],
        out_specs=pl.BlockSpec((tm, n), lambda i: (i, 0)),
        out_shape=jax.ShapeDtypeStruct((t, n), out_dtype),
        compiler_params=_params("parallel"),
        name="norm_matmul",
    )(x2d, g.reshape(1, d), w, b.reshape(1, n))


def _rope_tables(s):
    half = ROPE_DIMS // 2
    inv_freq = jnp.power(ROPE_THETA, -jnp.arange(half, dtype=F32) * 2.0 / ROPE_DIMS)
    ang = jnp.arange(s).astype(F32)[:, None] * inv_freq[None, :]
    cos, sin = jnp.cos(ang), jnp.sin(ang)
    ones = jnp.ones((s, MOBA_HEAD_DIM - ROPE_DIMS), F32)
    zeros = jnp.zeros((s, MOBA_HEAD_DIM - ROPE_DIMS), F32)
    zh = jnp.zeros((s, half), F32)
    c = jnp.concatenate([cos, cos, ones], axis=1)
    sa = jnp.concatenate([zh, sin, zeros], axis=1)
    sb = jnp.concatenate([-sin, zh, zeros], axis=1)
    rep = LANES // MOBA_HEAD_DIM
    return tuple(jnp.tile(a, (1, rep)) for a in (c, sa, sb))


def _moba_proj_kernel(x_ref, g_ref, w_ref, b_ref, c_ref, sa_ref, sb_ref,
                      q_ref, k_ref, vt_ref, km_ref):
    h = _rms(x_ref[...], g_ref[...]).astype(BF16)
    z = _dot(h, w_ref[...]) + b_ref[...]
    tm = z.shape[0]
    rep = MIX // LANES
    c = jnp.concatenate([c_ref[...]] * rep, axis=1)
    sa = jnp.concatenate([sa_ref[...]] * rep, axis=1)
    sb = jnp.concatenate([sb_ref[...]] * rep, axis=1)
    half = ROPE_DIMS // 2

    def rot(t):
        return t * c + pltpu.roll(t, half, 1) * sa + pltpu.roll(t, MIX - half, 1) * sb

    q = rot(z[:, :MIX]) * (MOBA_HEAD_DIM ** -0.5 * LOG2E)
    k = rot(z[:, MIX:2 * MIX])
    q_ref[...] = q.astype(BF16)
    k_ref[...] = k.astype(BF16)
    nb = tm // MOBA_BLOCK
    for i in range(nb):
        rows = slice(i * MOBA_BLOCK, (i + 1) * MOBA_BLOCK)
        vt_ref[i] = z[rows, 2 * MIX:].T.astype(BF16)
        km_ref[i] = jnp.mean(k[rows, :], axis=0, keepdims=True)


def moba_proj(x2d, g, w, b, tables, s, *, tm):
    t, d = x2d.shape
    n = w.shape[1]
    c, sa, sb = tables
    spt = s // tm
    tab = pl.BlockSpec((tm, LANES), lambda i: (i % spt, 0))
    row = pl.BlockSpec((tm, MIX), lambda i: (i, 0))
    nb = tm // MOBA_BLOCK
    return pl.pallas_call(
        _moba_proj_kernel,
        grid=(t // tm,),
        in_specs=[pl.BlockSpec((tm, d), lambda i: (i, 0)), _resident((1, d)),
                  _resident((d, n)), _resident((1, n)), tab, tab, tab],
        out_specs=[row, row, pl.BlockSpec((nb, MIX, MOBA_BLOCK), lambda i: (i, 0, 0)),
                   pl.BlockSpec((nb, 1, MIX), lambda i: (i, 0, 0))],
        out_shape=[jax.ShapeDtypeStruct((t, MIX), BF16)] * 2
        + [jax.ShapeDtypeStruct((t // MOBA_BLOCK, MIX, MOBA_BLOCK), BF16),
           jax.ShapeDtypeStruct((t // MOBA_BLOCK, 1, MIX), F32)],
        compiler_params=_params("parallel"),
        name="moba_proj",
    )(x2d, g.reshape(1, d), w, b.reshape(1, n), c, sa, sb)


MOBA_TQ = MOBA_BLOCK
MOBA_LANES = 256
HEADS_PER_STEP = MOBA_LANES // MOBA_HEAD_DIM
SUM_ROWS = 16


def _moba_attn_kernel(q_ref, k_ref, vt_ref, km_ref, o_ref, bias_ref, qh_ref, s_ref, p_ref):
    cur = pl.program_id(2)
    nb = km_ref.shape[1]
    hd = MOBA_HEAD_DIM
    neg = -jnp.inf
    own = pl.multiple_of(cur * MOBA_BLOCK, MOBA_BLOCK)
    nh = HEADS_PER_STEP

    ones_rows = jnp.ones((SUM_ROWS, MOBA_BLOCK), BF16)

    def v_rows(blk, hh):
        return jnp.concatenate([vt_ref[0, blk, hh * hd:(hh + 1) * hd, :], ones_rows], axis=0)

    def scores_into(slot, blk):
        off = pl.multiple_of(blk * MOBA_BLOCK, MOBA_BLOCK)
        kb = k_ref[0, pl.ds(off, MOBA_BLOCK), :]
        for hh in range(nh):
            s_ref[slot, hh] = _dot(kb, qh_ref[hh], NT)

    q_all = q_ref[0]
    lane = lax.broadcasted_iota(I32, q_all.shape, 1)
    km = km_ref[0]
    km_hi = km.astype(BF16)
    km_lo = (km - km_hi.astype(F32)).astype(BF16)
    k_own = k_ref[0, pl.ds(own, MOBA_BLOCK), :]
    init = []
    for hh in range(nh):
        q = jnp.where((lane >= hh * hd) & (lane < (hh + 1) * hd), q_all, jnp.zeros_like(q_all))
        qh_ref[hh] = q
        gate = _dot(km_hi, q, NT) + _dot(km_lo, q, NT)
        blk = lax.broadcasted_iota(I32, gate.shape, 0)
        gate = jnp.where(blk < cur, gate, neg)
        sel = jnp.zeros(gate.shape, F32)
        for _ in range(MOBA_TOPK):
            m = jnp.max(gate, axis=0, keepdims=True)
            first = jnp.min(jnp.where(gate == m, blk, nb), axis=0, keepdims=True)
            first = jnp.where(m > neg, first, nb)
            pick = blk == first
            sel = jnp.where(pick, 1.0, sel)
            gate = jnp.where(pick, neg, gate)
        bias_ref[hh] = jnp.where(sel > 0.0, 0.0, neg)

        s = _dot(k_own, q, NT)
        k_pos = lax.broadcasted_iota(I32, s.shape, 0)
        q_pos = lax.broadcasted_iota(I32, s.shape, 1)
        s = jnp.where(k_pos <= q_pos, s, neg)
        m0 = jnp.max(s, axis=0, keepdims=True)
        p = jnp.exp2(s - m0)
        p_ref[hh] = p.astype(BF16)
        init.append((m0, jnp.zeros((hd + SUM_ROWS, MOBA_TQ), F32), jnp.ones((1, MOBA_TQ), F32)))
    scores_into(0, 0)

    def step(j, slot, carry):
        pend = jnp.where(j == 0, cur, j - 1)
        pv = [_dot(v_rows(pend, hh), p_ref[hh]) for hh in range(nh)]
        out = []
        for hh in range(nh):
            m_i, acc, a_pend = carry[hh]
            s = s_ref[slot, hh] + bias_ref[hh, pl.ds(j, 1), :]
            m_new = jnp.maximum(m_i, jnp.max(s, axis=0, keepdims=True))
            alpha = jnp.exp2(m_i - m_new)
            acc_new = a_pend * acc + pv[hh]
            p_ref[hh] = jnp.exp2(s - m_new).astype(BF16)
            out.append((m_new, acc_new, alpha))
        scores_into(1 - slot, jnp.minimum(j + 1, nb - 1))
        return tuple(out)

    def body(i, carry):
        return step(2 * i + 1, 1, step(2 * i, 0, carry))

    trips = (cur + 1) // 2
    final = lax.fori_loop(0, trips, body, tuple(init))
    pend = jnp.where(trips == 0, cur, 2 * trips - 1)
    outs = []
    for hh in range(nh):
        _, acc, a_pend = final[hh]
        acc = a_pend * acc + _dot(v_rows(pend, hh), p_ref[hh])
        outs.append((acc[:hd] / acc[hd:hd + 1]).T)
    o_ref[0] = jnp.concatenate(outs, axis=1).astype(o_ref.dtype)


def moba_attention(q, k, vt, km):
    b, s, _ = q.shape
    nb = km.shape[1]
    qspec = pl.BlockSpec((1, MOBA_TQ, MOBA_LANES), lambda bi, hp, qi: (bi, qi, hp))
    return pl.pallas_call(
        _moba_attn_kernel,
        grid=(b, MIX // MOBA_LANES, s // MOBA_TQ),
        in_specs=[qspec,
                  pl.BlockSpec((1, s, MOBA_LANES), lambda bi, hp, qi: (bi, 0, hp)),
                  pl.BlockSpec((1, nb, MOBA_LANES, MOBA_BLOCK), lambda bi, hp, qi: (bi, 0, hp, 0)),
                  pl.BlockSpec((1, nb, MOBA_LANES), lambda bi, hp, qi: (bi, 0, hp))],
        out_specs=qspec,
        out_shape=jax.ShapeDtypeStruct((b, s, MIX), BF16),
        scratch_shapes=[pltpu.VMEM((HEADS_PER_STEP, nb, MOBA_TQ), F32),
                        pltpu.VMEM((HEADS_PER_STEP, MOBA_TQ, MOBA_LANES), BF16),
                        pltpu.VMEM((2, HEADS_PER_STEP, MOBA_BLOCK, MOBA_TQ), F32),
                        pltpu.VMEM((HEADS_PER_STEP, MOBA_BLOCK, MOBA_TQ), BF16)],
        compiler_params=_params("parallel", "parallel", "arbitrary"),
        name="moba_attention",
    )(q, k, vt, km)


SCAN_ROWS = 512


SUB = 128


def _split3(x):
    hi = x.astype(BF16)
    r1 = x - hi.astype(F32)
    mid = r1.astype(BF16)
    lo = (r1 - mid.astype(F32)).astype(BF16)
    return hi, mid, lo


def _dot_exact_lhs(lhs_bf16, x):
    return sum(_dot(lhs_bf16, t) for t in _split3(x))


def _chunk_masks(n):
    ri = lax.broadcasted_iota(I32, (n, n), 0)
    ci = lax.broadcasted_iota(I32, (n, n), 1)
    same = (ri // CHUNK) == (ci // CHUNK)
    return same, same & (ci <= ri)


def _gla_kernel(qk_ref, v_ref, r_ref, a_ref, gw_ref, gb_ref, ng_ref, o_ref, st_ref):
    @pl.when(pl.program_id(1) == 0)
    def _():
        st_ref[...] = jnp.zeros_like(st_ref)

    dk = GLA_DK // GLA_HEADS
    dv = GLA_DV // GLA_HEADS
    L = CHUNK
    same, causal = _chunk_masks(SUB)
    tri = jnp.where(causal, 1.0, 0.0).astype(BF16)
    blk = jnp.where(same, 1.0, 0.0).astype(BF16)
    lane = lax.broadcasted_iota(I32, (SUB, LANES), 1)
    gw = gw_ref[...]
    gb = gb_ref[...]
    ng = ng_ref[...]
    nsub = SCAN_ROWS // SUB
    subs = [slice(i * SUB, (i + 1) * SUB) for i in range(nsub)]
    heads = range(GLA_HEADS)
    chunks = [slice(c * L, (c + 1) * L) for c in range(SUB // L)]
    pair = lambda h: slice((h // 2) * LANES, (h // 2 + 1) * LANES)

    log_a = [_log_sigmoid(_dot_f32(a_ref[rows, :], gw) + gb) / GLA_TAU for rows in subs]
    cum = [_dot_exact_lhs(tri, x) for x in log_a]
    tot = [_dot_exact_lhs(blk, x) for x in log_a]
    q_dec, k_inv, k_end, dec, v = [], [], [], [], []
    for i, rows in enumerate(subs):
        qk = qk_ref[rows, :]
        k = qk[:, GLA_DK:]
        q_dec.append((qk[:, :GLA_DK] * (dk ** -0.5) * jnp.exp(cum[i])).astype(BF16))
        k_inv.append((k * jnp.exp(-cum[i])).astype(BF16))
        k_end.append((k * jnp.exp(tot[i] - cum[i])).astype(BF16))
        dec.append(jnp.exp(tot[i]))
        v.append(v_ref[rows, :].astype(BF16))

    def q_masked(i, h):
        lo_lane = (h % 2) * dk
        return jnp.where((lane >= lo_lane) & (lane < lo_lane + dk), q_dec[i][:, pair(h)],
                         jnp.zeros((SUB, LANES), BF16))

    qm = [[q_masked(i, h) for h in heads] for i in range(nsub)]
    vh = [[v[i][:, h * dv:(h + 1) * dv] for h in heads] for i in range(nsub)]
    att = [[_dot(qm[i][h], k_inv[i][:, pair(h)], NT) for h in heads] for i in range(nsub)]
    kv_t = [[[_dot(vh[i][h][cr], k_end[i][cr, pair(h)], TN) for cr in chunks] for h in heads]
            for i in range(nsub)]
    intra = [[_dot(jnp.where(causal, att[i][h], 0.0).astype(BF16), vh[i][h]) for h in heads]
             for i in range(nsub)]

    prev = [[[None] * len(chunks) for _ in heads] for _ in range(nsub)]
    for h in heads:
        state = st_ref[h]
        for i in range(nsub):
            for c in range(len(chunks)):
                prev[i][h][c] = state.astype(BF16)
                state = state * dec[i][c * L:c * L + 1, pair(h)] + kv_t[i][h][c]
        st_ref[h] = state

    for i, rows in enumerate(subs):
        outs = []
        for h in heads:
            inter = [_dot(qm[i][h][cr], prev[i][h][c], NT) for c, cr in enumerate(chunks)]
            o = intra[i][h] + jnp.concatenate(inter, axis=0)
            outs.append(o * lax.rsqrt(jnp.mean(o * o, axis=-1, keepdims=True) + EPS))
        r = r_ref[rows, :]
        o_ref[rows, :] = (jnp.concatenate(outs, axis=1) * ng * (r * _sigmoid(r))).astype(o_ref.dtype)


def gla_scan(z, gate_w_pad, gate_b, norm_g, b, s):
    t = z.shape[0]
    nc = s // SCAN_ROWS
    wide = lambda col: pl.BlockSpec((SCAN_ROWS, MIX), lambda bi, c: (bi * nc + c, col))
    dk = GLA_DK // GLA_HEADS
    dv = GLA_DV // GLA_HEADS
    return pl.pallas_call(
        _gla_kernel,
        grid=(b, nc),
        in_specs=[wide(0), wide(1), wide(2),
                  pl.BlockSpec((SCAN_ROWS, LANES), lambda bi, c: (bi * nc + c, 3 * MIX // LANES)),
                  _resident((LANES, GLA_DK)), _resident((1, GLA_DK)), _resident((1, GLA_DV))],
        out_specs=pl.BlockSpec((SCAN_ROWS, MIX), lambda bi, c: (bi * nc + c, 0)),
        out_shape=jax.ShapeDtypeStruct((t, MIX), BF16),
        scratch_shapes=[pltpu.VMEM((GLA_HEADS, dv, LANES), F32)],
        compiler_params=_params("parallel", "arbitrary"),
        name="gla_scan",
    )(z, z, z, z, gate_w_pad, gate_b.reshape(1, GLA_DK), norm_g.reshape(1, GLA_DV))


TAIL = 8


def _mlstm_kernel(qk_ref, v_ref, og_ref, ifc_ref, ifr_ref, cw_ref, cb_ref, o_ref,
                  tail_ref, ct_ref, n_ref, m_ref, qkc_ref):
    @pl.when(pl.program_id(1) == 0)
    def _():
        tail_ref[...] = jnp.zeros_like(tail_ref)
        ct_ref[...] = jnp.zeros_like(ct_ref)
        n_ref[...] = jnp.zeros_like(n_ref)
        m_ref[...] = jnp.zeros_like(m_ref)

    nh = MLSTM_HEADS
    dk = MLSTM_DQK // nh
    dv = MLSTM_DV // nh
    L = CHUNK
    rows_n = qk_ref.shape[0]
    neg = -jnp.inf

    x = qk_ref[...]
    xx = jnp.concatenate([tail_ref[...], x], axis=0)
    cw = cw_ref[...]
    y = cb_ref[...] + sum(
        cw[w:w + 1, :] * xx[TAIL - (MLSTM_CONV - 1) + w:TAIL - (MLSTM_CONV - 1) + w + rows_n, :]
        for w in range(MLSTM_CONV))
    tail_ref[...] = x[rows_n - TAIL:, :]
    qkc_ref[...] = y * _sigmoid(y)

    same, causal = _chunk_masks(SUB)
    tri = jnp.where(causal, 1.0, 0.0).astype(BF16)
    blk = jnp.where(same, 1.0, 0.0).astype(BF16)
    ri = lax.broadcasted_iota(I32, (SUB, SUB), 0)
    ci = lax.broadcasted_iota(I32, (SUB, SUB), 1)
    tri_t = jnp.where(same & (ri <= ci), 1.0, 0.0).astype(BF16)

    nsub = rows_n // SUB
    subs = [slice(i * SUB, (i + 1) * SUB) for i in range(nsub)]
    heads = range(nh)
    chunks = [slice(c * L, (c + 1) * L) for c in range(SUB // L)]
    nck = len(chunks)
    hcol = lambda h: slice(h * dk, (h + 1) * dk)

    ifc = [ifc_ref[rows, :] for rows in subs]
    ifr = [ifr_ref[i] for i in range(nsub)]
    lf_c = [_log_sigmoid(t) for t in ifc]
    cum_c = [_dot_exact_lhs(tri, t) for t in lf_c]
    tot_c = [_dot_exact_lhs(blk, t) for t in lf_c]
    cum_r = [sum(_dot(t, tri_t) for t in _split3(_log_sigmoid(u))) for u in ifr]

    a_end = [[tot_c[i][:, nh + h:nh + h + 1] - cum_c[i][:, nh + h:nh + h + 1] + ifc[i][:, h:h + 1]
              for h in heads] for i in range(nsub)]
    m_prev = [[[None] * nck for _ in heads] for _ in range(nsub)]
    m_next = [[[None] * nck for _ in heads] for _ in range(nsub)]
    for h in heads:
        m_run = m_ref[h:h + 1, 0:1]
        for i in range(nsub):
            for c, cr in enumerate(chunks):
                tot = tot_c[i][c * L:c * L + 1, nh + h:nh + h + 1]
                m_prev[i][h][c] = m_run
                m_run = jnp.maximum(tot + m_run, jnp.max(a_end[i][h][cr], axis=0, keepdims=True))
                m_next[i][h][c] = m_run
        m_ref[h:h + 1, :] = jnp.broadcast_to(m_run, (1, LANES))

    def per_row(vals):
        return jnp.concatenate([jnp.broadcast_to(t, (L, 1)) for t in vals], axis=0)

    qf = [[qkc_ref[rows, hcol(h)] for h in heads] for rows in subs]
    kf = [[qkc_ref[rows, MLSTM_DQK + h * dk:MLSTM_DQK + (h + 1) * dk] * (dk ** -0.5) for h in heads]
          for rows in subs]
    q = [[t.astype(BF16) for t in r] for r in qf]
    k = [[t.astype(BF16) for t in r] for r in kf]
    v = [[v_ref[rows, h * dv:(h + 1) * dv].astype(BF16) for h in heads] for rows in subs]
    qk = [[_dot(q[i][h], k[i][h], NT) for h in heads] for i in range(nsub)]

    kv_t = [[[None] * nck for _ in heads] for _ in range(nsub)]
    n_add = [[[None] * nck for _ in heads] for _ in range(nsub)]
    dec = [[[None] * nck for _ in heads] for _ in range(nsub)]
    for i in range(nsub):
        for h in heads:
            w = jnp.exp(a_end[i][h] - per_row(m_next[i][h]))
            wk = w * kf[i][h]
            for c, cr in enumerate(chunks):
                tot = tot_c[i][c * L:c * L + 1, nh + h:nh + h + 1]
                dec[i][h][c] = jnp.exp(tot + m_prev[i][h][c] - m_next[i][h][c])
                kv_t[i][h][c] = _dot(v[i][h][cr], wk[cr].astype(BF16), TN)
                n_add[i][h][c] = jnp.sum(wk[cr], axis=0, keepdims=True)

    w_inter = [[None] * nh for _ in range(nsub)]
    m_row = [[None] * nh for _ in range(nsub)]
    w_intra = [[None] * nh for _ in range(nsub)]
    for i in range(nsub):
        for h in heads:
            cum_i = cum_c[i][:, nh + h:nh + h + 1]
            log_d = jnp.where(causal, cum_i - cum_r[i][nh + h:nh + h + 1, :] + ifr[i][h:h + 1, :], neg)
            log_inter = cum_i + per_row(m_prev[i][h])
            m_i = jnp.maximum(log_inter, jnp.max(log_d, axis=1, keepdims=True))
            w_intra[i][h] = qk[i][h] * jnp.exp(log_d - m_i)
            w_inter[i][h] = jnp.exp(log_inter - m_i)
            m_row[i][h] = m_i
    num_intra = [[_dot(w_intra[i][h].astype(BF16), v[i][h]) for h in heads] for i in range(nsub)]

    c_prev = [[[None] * nck for _ in heads] for _ in range(nsub)]
    n_prev = [[[None] * nck for _ in heads] for _ in range(nsub)]
    for h in heads:
        ct = ct_ref[h]
        nv = n_ref[h:h + 1, :]
        for i in range(nsub):
            for c in range(nck):
                c_prev[i][h][c] = ct.astype(BF16)
                n_prev[i][h][c] = nv
                ct = dec[i][h][c] * ct + kv_t[i][h][c]
                nv = dec[i][h][c] * nv + n_add[i][h][c]
        ct_ref[h] = ct
        n_ref[h:h + 1, :] = nv

    for i, rows in enumerate(subs):
        outs = []
        for h in heads:
            inter = jnp.concatenate([_dot(q[i][h][cr], c_prev[i][h][c], NT)
                                     for c, cr in enumerate(chunks)], axis=0)
            qn = jnp.concatenate([jnp.sum(qf[i][h][cr] * n_prev[i][h][c], axis=1, keepdims=True)
                                  for c, cr in enumerate(chunks)], axis=0)
            num = w_inter[i][h] * inter + num_intra[i][h]
            den = w_inter[i][h] * qn + jnp.sum(w_intra[i][h], axis=1, keepdims=True)
            outs.append(num / jnp.maximum(jnp.abs(den), jnp.exp(-m_row[i][h])))
        og = og_ref[rows, :]
        o_ref[rows, :] = (jnp.concatenate(outs, axis=1) * _sigmoid(og)).astype(o_ref.dtype)


def mlstm_scan(z, ifr, conv_w, conv_b, b, s):
    t = z.shape[0]
    nc = s // SCAN_ROWS
    cpt = SCAN_ROWS // SUB
    dk = MLSTM_DQK // MLSTM_HEADS
    dv = MLSTM_DV // MLSTM_HEADS
    row = lambda w, col: pl.BlockSpec((SCAN_ROWS, w), lambda bi, c: (bi * nc + c, col))
    return pl.pallas_call(
        _mlstm_kernel,
        grid=(b, nc),
        in_specs=[row(2 * MLSTM_DQK, 0), row(MIX, 2), row(MIX, 3),
                  row(LANES, 4 * MIX // LANES),
                  pl.BlockSpec((cpt, 2 * MLSTM_HEADS, SUB), lambda bi, c: (bi * nc + c, 0, 0)),
                  _resident((MLSTM_CONV, 2 * MLSTM_DQK)), _resident((1, 2 * MLSTM_DQK))],
        out_specs=row(MIX, 0),
        out_shape=jax.ShapeDtypeStruct((t, MIX), BF16),
        scratch_shapes=[pltpu.VMEM((TAIL, 2 * MLSTM_DQK), F32),
                        pltpu.VMEM((MLSTM_HEADS, dv, dk), F32),
                        pltpu.VMEM((8, dk), F32),
                        pltpu.VMEM((8, LANES), F32),
                        pltpu.VMEM((SCAN_ROWS, 2 * MLSTM_DQK), F32)],
        compiler_params=_params("parallel", "arbitrary"),
        name="mlstm_scan",
    )(z, z, z, z, ifr, conv_w, conv_b.reshape(1, -1))


def _merge_kernel(x_ref, g_ref, wg_ref, bg_ref, ya_ref, yb_ref, yc_ref,
                  wa_ref, wb_ref, wc_ref, wo_ref, o_ref):
    x = x_ref[...]
    h = _rms(x, g_ref[...]).astype(BF16)
    d = D_MODEL
    merged = None
    for i, (y_ref, w_ref) in enumerate(((ya_ref, wa_ref), (yb_ref, wb_ref), (yc_ref, wc_ref))):
        gate = _sigmoid(_dot(h, wg_ref[:, i * d:(i + 1) * d]) + bg_ref[:, i * d:(i + 1) * d])
        term = gate * _dot(y_ref[...], w_ref[...])
        merged = term if merged is None else merged + term
    o_ref[...] = x + _dot(merged.astype(BF16), wo_ref[...])


def merge_out(x2d, g, w_gate, b_gate, ya, yb, yc, wa, wb, wc, wo, *, tm):
    t, d = x2d.shape
    row = lambda w: pl.BlockSpec((tm, w), lambda i: (i, 0))
    return pl.pallas_call(
        _merge_kernel,
        grid=(t // tm,),
        in_specs=[row(d), _resident((1, d)), _resident((d, N_BRANCHES * d)),
                  _resident((1, N_BRANCHES * d)), row(MIX), row(MIX), row(MIX),
                  _resident((MIX, d)), _resident((MIX, d)), _resident((MIX, d)),
                  _resident((d, d))],
        out_specs=row(d),
        out_shape=jax.ShapeDtypeStruct((t, d), F32),
        compiler_params=_params("parallel"),
        name="merge_out",
    )(x2d, g.reshape(1, d), w_gate, b_gate.reshape(1, -1), ya, yb, yc, wa, wb, wc, wo)


def _ffn_kernel(x_ref, g_ref, wg_ref, wu_ref, wd_ref, o_ref, h_ref, acc_ref):
    f = pl.program_id(1)

    @pl.when(f == 0)
    def _():
        h_ref[...] = _rms(x_ref[...], g_ref[...]).astype(BF16)
        acc_ref[...] = jnp.zeros_like(acc_ref)

    h = h_ref[...]
    gt = _dot(h, wg_ref[...])
    up = _dot(h, wu_ref[...])
    act = (gt * _sigmoid(gt) * up).astype(BF16)
    acc_ref[...] += _dot(act, wd_ref[...])

    @pl.when(f == pl.num_programs(1) - 1)
    def _():
        o_ref[...] = x_ref[...] + acc_ref[...]


def _ff_tile(ff):
    for cand in (1408, 1024, 768, 512, 256, 128):
        if ff % cand == 0:
            return cand
    raise ValueError(f"unsupported d_ff {ff}")


def dense_ffn(x2d, g, wg, wu, wd, *, tm):
    t, d = x2d.shape
    ff = wg.shape[1]
    tf = _ff_tile(ff)
    return pl.pallas_call(
        _ffn_kernel,
        grid=(t // tm, ff // tf),
        in_specs=[pl.BlockSpec((tm, d), lambda i, f: (i, 0)), _resident((1, d)),
                  pl.BlockSpec((d, tf), lambda i, f: (0, f)),
                  pl.BlockSpec((d, tf), lambda i, f: (0, f)),
                  pl.BlockSpec((tf, d), lambda i, f: (f, 0))],
        out_specs=pl.BlockSpec((tm, d), lambda i, f: (i, 0)),
        out_shape=jax.ShapeDtypeStruct((t, d), F32),
        scratch_shapes=[pltpu.VMEM((tm, d), BF16), pltpu.VMEM((tm, d), F32)],
        compiler_params=_params("parallel", "arbitrary"),
        name="dense_ffn",
    )(x2d, g.reshape(1, d), wg, wu, wd)


ROUTER_TM = 512
MOE_ROWS = 512
META_E, META_G, META_RANK = 0, 2, 4


def _router_kernel(x_ref, g_ref, rw_ref, rb_ref, h_ref, meta_ref, cnt_ref, carry_ref):
    @pl.when(pl.program_id(0) == 0)
    def _():
        carry_ref[...] = jnp.zeros_like(carry_ref)

    h = _rms(x_ref[...], g_ref[...])
    h_ref[...] = h
    tm = h.shape[0]
    lane = lax.broadcasted_iota(I32, (tm, LANES), 1)
    neg = -jnp.inf
    logits = jnp.where(lane < N_EXPERTS, _dot_f32(h, rw_ref[...]) + rb_ref[...], neg)
    m1 = jnp.max(logits, axis=1, keepdims=True)
    e1 = jnp.min(jnp.where(logits == m1, lane, LANES), axis=1, keepdims=True)
    rest = jnp.where(lane == e1, neg, logits)
    m2 = jnp.max(rest, axis=1, keepdims=True)
    e2 = jnp.min(jnp.where(rest == m2, lane, LANES), axis=1, keepdims=True)
    ex = jnp.exp(m2 - m1)
    g1 = 1.0 / (1.0 + ex)
    g2 = ex / (1.0 + ex)
    oh1 = jnp.where(lane == e1, 1.0, 0.0)
    oh2 = jnp.where(lane == e2, 1.0, 0.0)
    cnt = oh1 + oh2
    ri = lax.broadcasted_iota(I32, (tm, tm), 0)
    ci = lax.broadcasted_iota(I32, (tm, tm), 1)
    strict = jnp.where(ci < ri, 1.0, 0.0).astype(BF16)
    excl = _dot(strict, cnt.astype(BF16)) + carry_ref[...]
    r1 = jnp.sum(excl * oh1, axis=1, keepdims=True)
    r2 = jnp.sum(excl * oh2, axis=1, keepdims=True)
    carry_ref[...] += jnp.sum(cnt, axis=0, keepdims=True)
    cnt_ref[...] = carry_ref[...]
    meta = jnp.zeros((tm, LANES), F32)
    for pos, val in ((META_E, e1.astype(F32)), (META_E + 1, e2.astype(F32)),
                     (META_G, g1), (META_G + 1, g2), (META_RANK, r1), (META_RANK + 1, r2)):
        meta = jnp.where(lane == pos, val, meta)
    meta_ref[...] = meta


def moe_router(x2d, g, rw_pad, rb_pad):
    t, d = x2d.shape
    tm = ROUTER_TM
    return pl.pallas_call(
        _router_kernel,
        grid=(t // tm,),
        in_specs=[pl.BlockSpec((tm, d), lambda i: (i, 0)), _resident((1, d)),
                  _resident((d, LANES)), _resident((1, LANES))],
        out_specs=[pl.BlockSpec((tm, d), lambda i: (i, 0)),
                   pl.BlockSpec((tm, LANES), lambda i: (i, 0)),
                   pl.BlockSpec((1, LANES), lambda i: (0, 0))],
        out_shape=[jax.ShapeDtypeStruct((t, d), F32), jax.ShapeDtypeStruct((t, LANES), F32),
                   jax.ShapeDtypeStruct((1, LANES), F32)],
        scratch_shapes=[pltpu.VMEM((1, LANES), F32)],
        compiler_params=_params("arbitrary"),
        name="moe_router",
    )(x2d, g.reshape(1, d), rw_pad, rb_pad)


GATHER_ROWS = 2048


def _row_copy(src_hbm, dst_ref, src_row, dst_row, sem):
    return pltpu.make_async_copy(src_hbm.at[pl.ds(src_row, 1)], dst_ref.at[pl.ds(dst_row, 1)], sem)


ISSUE_UNROLL = 8


def _gather_kernel(idx_ref, src_hbm, o_ref, sem):
    n = o_ref.shape[0]

    def start(i, _):
        for u in range(ISSUE_UNROLL):
            r = i * ISSUE_UNROLL + u
            _row_copy(src_hbm, o_ref, idx_ref[0, 0, r], r, sem).start(priority=u % 2)
        return 0

    def wait(i, _):
        for u in range(ISSUE_UNROLL):
            _row_copy(src_hbm, o_ref, 0, i * ISSUE_UNROLL + u, sem).wait()
        return 0

    lax.fori_loop(0, n // ISSUE_UNROLL, start, 0)
    lax.fori_loop(0, n // ISSUE_UNROLL, wait, 0)


def gather_rows(src, idx):
    n = idx.shape[0]
    d = src.shape[1]
    nblk = n // GATHER_ROWS
    return pl.pallas_call(
        _gather_kernel,
        grid=(nblk,),
        in_specs=[pl.BlockSpec((1, 1, GATHER_ROWS), lambda i: (i, 0, 0), memory_space=pltpu.SMEM),
                  pl.BlockSpec(memory_space=pl.ANY)],
        out_specs=pl.BlockSpec((GATHER_ROWS, d), lambda i: (i, 0)),
        out_shape=jax.ShapeDtypeStruct((n, d), src.dtype),
        scratch_shapes=[pltpu.SemaphoreType.DMA(())],
        compiler_params=_params("arbitrary"),
        name="gather_rows",
    )(idx.reshape(nblk, 1, GATHER_ROWS), src)


def _grouped_ffn_kernel(be_ref, bv_ref, x_ref, wg_ref, wu_ref, wd_ref, o_ref, acc_ref):
    i = pl.program_id(0)
    f = pl.program_id(1)

    @pl.when(f == 0)
    def _():
        acc_ref[...] = jnp.zeros_like(acc_ref)

    @pl.when(bv_ref[i] > 0)
    def _():
        h = x_ref[...].astype(BF16)
        gt = _dot(h, wg_ref[...])
        up = _dot(h, wu_ref[...])
        act = (gt * _sigmoid(gt) * up).astype(BF16)
        acc_ref[...] += _dot(act, wd_ref[...])

    @pl.when(f == pl.num_programs(1) - 1)
    def _():
        o_ref[...] = acc_ref[...]


def grouped_ffn(xs, blk_e, blk_valid, wg, wu, wd):
    n, d = xs.shape
    ff = wg.shape[2]
    tf = _ff_tile(ff)
    nblk = n // MOE_ROWS
    grid_spec = pltpu.PrefetchScalarGridSpec(
        num_scalar_prefetch=2,
        grid=(nblk, ff // tf),
        in_specs=[pl.BlockSpec((MOE_ROWS, d), lambda i, f, be, bv: (i, 0)),
                  pl.BlockSpec((None, d, tf), lambda i, f, be, bv: (be[i], 0, f)),
                  pl.BlockSpec((None, d, tf), lambda i, f, be, bv: (be[i], 0, f)),
                  pl.BlockSpec((None, tf, d), lambda i, f, be, bv: (be[i], f, 0))],
        out_specs=pl.BlockSpec((MOE_ROWS, d), lambda i, f, be, bv: (i, 0)),
        scratch_shapes=[pltpu.VMEM((MOE_ROWS, d), F32)],
    )
    return pl.pallas_call(
        _grouped_ffn_kernel,
        grid_spec=grid_spec,
        out_shape=jax.ShapeDtypeStruct((n, d), F32),
        compiler_params=_params("parallel", "arbitrary"),
        name="grouped_ffn",
    )(blk_e, blk_valid, xs, wg, wu, wd)


COMBINE_TM = 1024


def _combine_kernel(d1_ref, d2_ref, x_ref, meta_ref, fg_ref, ys_hbm, o_ref, b1_ref, b2_ref, sem):
    n = x_ref.shape[0]

    def start(i, _):
        for u in range(ISSUE_UNROLL // 2):
            r = i * (ISSUE_UNROLL // 2) + u
            _row_copy(ys_hbm, b1_ref, d1_ref[0, 0, r], r, sem.at[0]).start(priority=0)
            _row_copy(ys_hbm, b2_ref, d2_ref[0, 0, r], r, sem.at[1]).start(priority=1)
        return 0

    def wait(i, _):
        for u in range(ISSUE_UNROLL // 2):
            r = i * (ISSUE_UNROLL // 2) + u
            _row_copy(ys_hbm, b1_ref, 0, r, sem.at[0]).wait()
            _row_copy(ys_hbm, b2_ref, 0, r, sem.at[1]).wait()
        return 0

    lax.fori_loop(0, n // (ISSUE_UNROLL // 2), start, 0)
    lax.fori_loop(0, n // (ISSUE_UNROLL // 2), wait, 0)
    meta = meta_ref[...]
    g1 = meta[:, META_G:META_G + 1]
    g2 = meta[:, META_G + 1:META_G + 2]
    y = x_ref[...] + (g1 * b1_ref[...] + g2 * b2_ref[...])
    o_ref[...] = _rms(y, fg_ref[...])


def moe_combine_norm(x2d, meta, dest1, dest2, ys, final_g):
    t, d = x2d.shape
    tm = COMBINE_TM
    nblk = t // tm
    idx = pl.BlockSpec((1, 1, tm), lambda i: (i, 0, 0), memory_space=pltpu.SMEM)
    return pl.pallas_call(
        _combine_kernel,
        grid=(nblk,),
        in_specs=[idx, idx, pl.BlockSpec((tm, d), lambda i: (i, 0)),
                  pl.BlockSpec((tm, LANES), lambda i: (i, 0)), _resident((1, d)),
                  pl.BlockSpec(memory_space=pl.ANY)],
        out_specs=pl.BlockSpec((tm, d), lambda i: (i, 0)),
        out_shape=jax.ShapeDtypeStruct((t, d), F32),
        scratch_shapes=[pltpu.VMEM((tm, d), F32), pltpu.VMEM((tm, d), F32),
                        pltpu.SemaphoreType.DMA((2,))],
        compiler_params=_params("arbitrary"),
        name="moe_combine_norm",
    )(dest1.reshape(nblk, 1, tm), dest2.reshape(nblk, 1, tm), x2d, meta,
      final_g.reshape(1, d), ys)


def moe_layer_final(x2d, norm_g, router_w, router_b, wg, wu, wd, final_g):
    t, d = x2d.shape
    rw_pad = jnp.pad(router_w, ((0, 0), (0, LANES - N_EXPERTS)))
    rb_pad = jnp.pad(router_b, (0, LANES - N_EXPERTS)).reshape(1, LANES)
    h, meta, counts = moe_router(x2d, norm_g, rw_pad, rb_pad)

    counts = counts[0, :N_EXPERTS].astype(I32)
    padded = (counts + MOE_ROWS - 1) // MOE_ROWS * MOE_ROWS
    pad_ends = jnp.cumsum(padded)
    pad_starts = pad_ends - padded
    e = meta[:, META_E:META_E + TOP_K].astype(I32)
    rank = meta[:, META_RANK:META_RANK + TOP_K].astype(I32)
    dest = pad_starts[e] + rank
    n_blk = (t * TOP_K) // MOE_ROWS + N_EXPERTS
    n_rows = n_blk * MOE_ROWS
    tok = jnp.broadcast_to(jnp.arange(t, dtype=I32)[:, None], (t, TOP_K))
    rows_tok = jnp.zeros((n_rows,), I32).at[dest.reshape(-1)].set(tok.reshape(-1))
    blk_start = jnp.arange(n_blk, dtype=I32) * MOE_ROWS
    blk_valid = (blk_start < pad_ends[-1]).astype(I32)
    blk_e = jnp.minimum(jnp.sum((blk_start[:, None] >= pad_ends[None, :]).astype(I32), axis=1),
                        N_EXPERTS - 1)

    xs = gather_rows(h, rows_tok)
    ys = grouped_ffn(xs, blk_e, blk_valid, wg, wu, wd)
    return moe_combine_norm(x2d, meta, dest[:, 0], dest[:, 1], ys, final_g)


def _col_slices():
    sizes = (('moba', 3 * MIX), ('gla_qkv', 2 * GLA_DK + GLA_DV), ('gla_a', GLA_GATE_RANK),
             ('gla_r', GLA_DV), ('mlstm_qkv', 2 * MLSTM_DQK + MLSTM_DV),
             ('mlstm_if', 2 * MLSTM_HEADS), ('mlstm_o', MLSTM_DV),
             ('merge_gate', N_BRANCHES * D_MODEL))
    out, off = {}, 0
    for name, size in sizes:
        out[name] = slice(off, off + size)
        off += size
    return out


def _pad_cols(a, width):
    return jnp.pad(a, ((0, 0), (0, width - a.shape[1])))


def hybrid_mixer_layer(x2d, b, s, norm_g, w_in, b_in, conv_w, conv_b, gate_w, gate_b, gla_norm_g,
                       w_up_moba, w_up_gla, w_up_mlstm, w_o, tables):
    cs = _col_slices()
    bias = b_in.reshape(1, -1)
    wcat = lambda names: jnp.concatenate([w_in[:, cs[n]] for n in names], axis=1)
    bcat = lambda names: jnp.concatenate([bias[:, cs[n]] for n in names], axis=1)

    nb = s // MOBA_BLOCK
    q, k, vt, km = moba_proj(x2d, norm_g, w_in[:, cs['moba']].astype(BF16), bias[:, cs['moba']],
                             tables, s, tm=512)
    y_moba = moba_attention(q.reshape(b, s, MIX), k.reshape(b, s, MIX),
                            vt.reshape(b, nb, MIX, MOBA_BLOCK),
                            km.reshape(b, nb, MIX)).reshape(b * s, MIX)

    w_gla = jnp.concatenate([wcat(['gla_qkv', 'gla_r']), _pad_cols(w_in[:, cs['gla_a']], LANES)], axis=1)
    b_gla = jnp.concatenate([bcat(['gla_qkv', 'gla_r']), _pad_cols(bias[:, cs['gla_a']], LANES)], axis=1)
    z_gla = norm_matmul(x2d, norm_g, w_gla.astype(BF16), b_gla, tm=512)
    gate_w_pad = jnp.pad(gate_w, ((0, LANES - GLA_GATE_RANK), (0, 0)))
    y_gla = gla_scan(z_gla, gate_w_pad, gate_b, gla_norm_g, b, s)

    w_ml = jnp.concatenate([wcat(['mlstm_qkv', 'mlstm_o']), _pad_cols(w_in[:, cs['mlstm_if']], LANES)], axis=1)
    b_ml = jnp.concatenate([bcat(['mlstm_qkv', 'mlstm_o']), _pad_cols(bias[:, cs['mlstm_if']], LANES)], axis=1)
    z_ml = norm_matmul(x2d, norm_g, w_ml.astype(BF16), b_ml, tm=512)
    n_if = 2 * MLSTM_HEADS
    ifr = z_ml[:, 4 * MIX:4 * MIX + n_if].reshape(b * s // SUB, SUB, n_if).transpose(0, 2, 1)
    y_ml = mlstm_scan(z_ml, ifr, conv_w, conv_b, b, s)

    return merge_out(x2d, norm_g, w_in[:, cs['merge_gate']].astype(BF16), bias[:, cs['merge_gate']],
                     y_moba, y_gla, y_ml, w_up_moba.astype(BF16), w_up_gla.astype(BF16),
                     w_up_mlstm.astype(BF16), w_o.astype(BF16), tm=512)


def kernel(x, attn_norm_g, w_in, b_in, mlstm_conv_w, mlstm_conv_b, gla_gate_w, gla_gate_b, gla_norm_g, w_up_moba, w_up_gla, w_up_mlstm, w_o, ffn_norm_g, ffn_w_gate, ffn_w_up, ffn_w_down, router_w, router_b, moe_w_gate, moe_w_up, moe_w_down, final_norm_g):
    b, s, d = x.shape
    depth = w_in.shape[0]
    assert d == D_MODEL and depth == 2, "two layers: dense SwiGLU then MoE"
    assert s % SCAN_ROWS == 0 and s % MOBA_BLOCK == 0
    x2d = x.reshape(b * s, d)
    tables = _rope_tables(s)
    for l in range(depth):
        x2d = hybrid_mixer_layer(x2d, b, s, attn_norm_g[l], w_in[l], b_in[l], mlstm_conv_w[l],
                                 mlstm_conv_b[l], gla_gate_w[l], gla_gate_b[l], gla_norm_g[l],
                                 w_up_moba[l], w_up_gla[l], w_up_mlstm[l], w_o[l], tables)
        j = l // 2
        if l % 2 == 0:
            x2d = dense_ffn(x2d, ffn_norm_g[l], ffn_w_gate[j].astype(BF16), ffn_w_up[j].astype(BF16),
                            ffn_w_down[j].astype(BF16), tm=512)
        else:
            x2d = moe_layer_final(x2d, ffn_norm_g[l], router_w[j], router_b[j],
                                  moe_w_gate[j].astype(BF16), moe_w_up[j].astype(BF16),
                                  moe_w_down[j].astype(BF16), final_norm_g)
    return x2d.reshape(b, s, d)
```

```python
import functools

import jax
import jax.numpy as jnp
from jax import lax
from jax.experimental import pallas as pl
from jax.experimental.pallas import tpu as pltpu

F32 = jnp.float32
BF16 = jnp.bfloat16
I32 = jnp.int32

D_MODEL = 1024
MIX = D_MODEL // 2
MOBA_HEAD_DIM = 64
MOBA_BLOCK = 256
MOBA_TOPK = 3
ROPE_THETA = 500000.0
ROPE_DIMS = MOBA_HEAD_DIM // 4
GLA_HEADS = 4
GLA_DK = MIX // 2
GLA_DV = MIX
GLA_GATE_RANK = 16
GLA_TAU = 16.0
MLSTM_HEADS = 4
MLSTM_DQK = MIX
MLSTM_DV = MIX
MLSTM_CONV = 4
CHUNK = 64
N_BRANCHES = 3
N_EXPERTS = 8
TOP_K = 2
EPS = 1e-6
LOG2E = 1.4426950408889634

LANES = 128
VMEM_LIMIT = 56 * 1024 * 1024

NN = (((1,), (0,)), ((), ()))
NT = (((1,), (1,)), ((), ()))
TN = (((0,), (0,)), ((), ()))


def _dot(a, b, dims=NN):
    return lax.dot_general(a, b, dims, preferred_element_type=F32)


def _dot_f32(a, b, dims=NN):
    return lax.dot_general(a, b, dims, preferred_element_type=F32,
                           precision=lax.Precision.HIGHEST)


def _rms(x, g):
    return x * lax.rsqrt(jnp.mean(x * x, axis=-1, keepdims=True) + EPS) * g


def _sigmoid(x):
    return 1.0 / (1.0 + jnp.exp(-x))


def _log_sigmoid(x):
    return jnp.minimum(x, 0.0) - jnp.log(1.0 + jnp.exp(-jnp.abs(x)))


def _params(*sem):
    return pltpu.CompilerParams(dimension_semantics=sem, vmem_limit_bytes=VMEM_LIMIT)


def _resident(shape):
    n = len(shape)
    return pl.BlockSpec(shape, lambda *_: (0,) * n, pipeline_mode=pl.Buffered(1))


def _norm_mm_kernel(x_ref, g_ref, w_ref, b_ref, o_ref):
    h = _rms(x_ref[...], g_ref[...]).astype(BF16)
    o_ref[...] = (_dot(h, w_ref[...]) + b_ref[...]).astype(o_ref.dtype)


def norm_matmul(x2d, g, w, b, *, tm, out_dtype=F32):
    t, d = x2d.shape
    n = w.shape[1]
    return pl.pallas_call(
        _norm_mm_kernel,
        grid=(t // tm,),
        in_specs=[pl.BlockSpec((tm, d), lambda i: (i, 0)), _resident((1, d)),
                  _resident((d, n)), _resident((1, n))],
        out_specs=pl.BlockSpec((tm, n), lambda i: (i, 0)),
        out_shape=jax.ShapeDtypeStruct((t, n), out_dtype),
        compiler_params=_params("parallel"),
        name="norm_matmul",
    )(x2d, g.reshape(1, d), w, b.reshape(1, n))


def _rope_tables(s):
    half = ROPE_DIMS // 2
    inv_freq = jnp.power(ROPE_THETA, -jnp.arange(half, dtype=F32) * 2.0 / ROPE_DIMS)
    ang = jnp.arange(s).astype(F32)[:, None] * inv_freq[None, :]
    cos, sin = jnp.cos(ang), jnp.sin(ang)
    ones = jnp.ones((s, MOBA_HEAD_DIM - ROPE_DIMS), F32)
    zeros = jnp.zeros((s, MOBA_HEAD_DIM - ROPE_DIMS), F32)
    zh = jnp.zeros((s, half), F32)
    c = jnp.concatenate([cos, cos, ones], axis=1)
    sa = jnp.concatenate([zh, sin, zeros], axis=1)
    sb = jnp.concatenate([-sin, zh, zeros], axis=1)
    rep = LANES // MOBA_HEAD_DIM
    return tuple(jnp.tile(a, (1, rep)) for a in (c, sa, sb))


def _moba_proj_kernel(x_ref, g_ref, w_ref, b_ref, c_ref, sa_ref, sb_ref,
                      q_ref, k_ref, vt_ref, km_ref):
    h = _rms(x_ref[...], g_ref[...]).astype(BF16)
    z = _dot(h, w_ref[...]) + b_ref[...]
    tm = z.shape[0]
    rep = MIX // LANES
    c = jnp.concatenate([c_ref[...]] * rep, axis=1)
    sa = jnp.concatenate([sa_ref[...]] * rep, axis=1)
    sb = jnp.concatenate([sb_ref[...]] * rep, axis=1)
    half = ROPE_DIMS // 2

    def rot(t):
        return t * c + pltpu.roll(t, half, 1) * sa + pltpu.roll(t, MIX - half, 1) * sb

    q = rot(z[:, :MIX]) * (MOBA_HEAD_DIM ** -0.5 * LOG2E)
    k = rot(z[:, MIX:2 * MIX])
    q_ref[...] = q.astype(BF16)
    k_ref[...] = k.astype(BF16)
    nb = tm // MOBA_BLOCK
    for i in range(nb):
        rows = slice(i * MOBA_BLOCK, (i + 1) * MOBA_BLOCK)
        vt_ref[i] = z[rows, 2 * MIX:].T.astype(BF16)
        km_ref[i] = jnp.mean(k[rows, :], axis=0, keepdims=True)


def moba_proj(x2d, g, w, b, tables, s, *, tm):
    t, d = x2d.shape
    n = w.shape[1]
    c, sa, sb = tables
    spt = s // tm
    tab = pl.BlockSpec((tm, LANES), lambda i: (i % spt, 0))
    row = pl.BlockSpec((tm, MIX), lambda i: (i, 0))
    nb = tm // MOBA_BLOCK
    return pl.pallas_call(
        _moba_proj_kernel,
        grid=(t // tm,),
        in_specs=[pl.BlockSpec((tm, d), lambda i: (i, 0)), _resident((1, d)),
                  _resident((d, n)), _resident((1, n)), tab, tab, tab],
        out_specs=[row, row, pl.BlockSpec((nb, MIX, MOBA_BLOCK), lambda i: (i, 0, 0)),
                   pl.BlockSpec((nb, 1, MIX), lambda i: (i, 0, 0))],
        out_shape=[jax.ShapeDtypeStruct((t, MIX), BF16)] * 2
        + [jax.ShapeDtypeStruct((t // MOBA_BLOCK, MIX, MOBA_BLOCK), BF16),
           jax.ShapeDtypeStruct((t // MOBA_BLOCK, 1, MIX), F32)],
        compiler_params=_params("parallel"),
        name="moba_proj",
    )(x2d, g.reshape(1, d), w, b.reshape(1, n), c, sa, sb)


MOBA_TQ = MOBA_BLOCK
MOBA_LANES = 256
HEADS_PER_STEP = MOBA_LANES // MOBA_HEAD_DIM
SUM_ROWS = 16


def _moba_attn_kernel(q_ref, k_ref, vt_ref, km_ref, o_ref, bias_ref, qh_ref, s_ref, p_ref):
    cur = pl.program_id(2)
    nb = km_ref.shape[1]
    hd = MOBA_HEAD_DIM
    neg = -jnp.inf
    own = pl.multiple_of(cur * MOBA_BLOCK, MOBA_BLOCK)
    nh = HEADS_PER_STEP

    ones_rows = jnp.ones((SUM_ROWS, MOBA_BLOCK), BF16)

    def v_rows(blk, hh):
        return jnp.concatenate([vt_ref[0, blk, hh * hd:(hh + 1) * hd, :], ones_rows], axis=0)

    def scores_into(slot, blk):
        off = pl.multiple_of(blk * MOBA_BLOCK, MOBA_BLOCK)
        kb = k_ref[0, pl.ds(off, MOBA_BLOCK), :]
        for hh in range(nh):
            s_ref[slot, hh] = _dot(kb, qh_ref[hh], NT)

    q_all = q_ref[0]
    lane = lax.broadcasted_iota(I32, q_all.shape, 1)
    km = km_ref[0]
    km_hi = km.astype(BF16)
    km_lo = (km - km_hi.astype(F32)).astype(BF16)
    k_own = k_ref[0, pl.ds(own, MOBA_BLOCK), :]
    k_first = k_ref[0, 0:MOBA_BLOCK, :]
    init = []
    for hh in range(nh):
        q = jnp.where((lane >= hh * hd) & (lane < (hh + 1) * hd), q_all, jnp.zeros_like(q_all))
        qh_ref[hh] = q
        lhs = jnp.concatenate([km_hi, km_lo, k_own, k_first], axis=0)
        prod = _dot(lhs, q, NT)
        gate = prod[:nb] + prod[nb:2 * nb]
        blk = lax.broadcasted_iota(I32, gate.shape, 0)
        gate = jnp.where(blk < cur, gate, neg)
        sel = jnp.zeros(gate.shape, F32)
        for _ in range(MOBA_TOPK):
            m = jnp.max(gate, axis=0, keepdims=True)
            first = jnp.min(jnp.where(gate == m, blk, nb), axis=0, keepdims=True)
            first = jnp.where(m > neg, first, nb)
            pick = blk == first
            sel = jnp.where(pick, 1.0, sel)
            gate = jnp.where(pick, neg, gate)
        bias_ref[hh] = jnp.where(sel > 0.0, 0.0, neg)

        s = prod[2 * nb:2 * nb + MOBA_BLOCK]
        s_ref[0, hh] = prod[2 * nb + MOBA_BLOCK:]
        k_pos = lax.broadcasted_iota(I32, s.shape, 0)
        q_pos = lax.broadcasted_iota(I32, s.shape, 1)
        s = jnp.where(k_pos <= q_pos, s, neg)
        m0 = jnp.max(s, axis=0, keepdims=True)
        p = jnp.exp2(s - m0)
        p_ref[hh] = p.astype(BF16)
        init.append((m0, jnp.zeros((hd + SUM_ROWS, MOBA_TQ), F32), jnp.ones((1, MOBA_TQ), F32)))

    def step(j, slot, carry):
        pend = jnp.where(j == 0, cur, j - 1)
        pv = [_dot(v_rows(pend, hh), p_ref[hh]) for hh in range(nh)]
        out = []
        for hh in range(nh):
            m_i, acc, a_pend = carry[hh]
            bias = bias_ref[hh, pl.ds(j, 1), :]
            m_new = jnp.maximum(m_i, jnp.max(s_ref[slot, hh], axis=0, keepdims=True) + bias)
            alpha = jnp.exp2(m_i - m_new)
            acc_new = a_pend * acc + pv[hh]
            p_ref[hh] = jnp.exp2(s_ref[slot, hh] + (bias - m_new)).astype(BF16)
            out.append((m_new, acc_new, alpha))
        scores_into(1 - slot, jnp.minimum(j + 1, nb - 1))
        return tuple(out)

    def body(i, carry):
        return step(2 * i + 1, 1, step(2 * i, 0, carry))

    trips = (cur + 1) // 2
    final = lax.fori_loop(0, trips, body, tuple(init))
    pend = jnp.where(trips == 0, cur, 2 * trips - 1)
    outs = []
    for hh in range(nh):
        _, acc, a_pend = final[hh]
        acc = a_pend * acc + _dot(v_rows(pend, hh), p_ref[hh])
        outs.append((acc[:hd] / acc[hd:hd + 1]).T)
    o_ref[0] = jnp.concatenate(outs, axis=1).astype(o_ref.dtype)


def moba_attention(q, k, vt, km):
    b, s, _ = q.shape
    nb = km.shape[1]
    qspec = pl.BlockSpec((1, MOBA_TQ, MOBA_LANES), lambda bi, hp, qi: (bi, qi, hp))
    return pl.pallas_call(
        _moba_attn_kernel,
        grid=(b, MIX // MOBA_LANES, s // MOBA_TQ),
        in_specs=[qspec,
                  pl.BlockSpec((1, s, MOBA_LANES), lambda bi, hp, qi: (bi, 0, hp)),
                  pl.BlockSpec((1, nb, MOBA_LANES, MOBA_BLOCK), lambda bi, hp, qi: (bi, 0, hp, 0)),
                  pl.BlockSpec((1, nb, MOBA_LANES), lambda bi, hp, qi: (bi, 0, hp))],
        out_specs=qspec,
        out_shape=jax.ShapeDtypeStruct((b, s, MIX), BF16),
        scratch_shapes=[pltpu.VMEM((HEADS_PER_STEP, nb, MOBA_TQ), F32),
                        pltpu.VMEM((HEADS_PER_STEP, MOBA_TQ, MOBA_LANES), BF16),
                        pltpu.VMEM((2, HEADS_PER_STEP, MOBA_BLOCK, MOBA_TQ), F32),
                        pltpu.VMEM((HEADS_PER_STEP, MOBA_BLOCK, MOBA_TQ), BF16)],
        compiler_params=_params("parallel", "parallel", "arbitrary"),
        name="moba_attention",
    )(q, k, vt, km)


SCAN_ROWS = 512


SUB = 128


def _split3(x):
    hi = x.astype(BF16)
    r1 = x - hi.astype(F32)
    mid = r1.astype(BF16)
    lo = (r1 - mid.astype(F32)).astype(BF16)
    return hi, mid, lo


def _dot_exact_lhs(lhs_bf16, x):
    return sum(_dot(lhs_bf16, t) for t in _split3(x))


def _chunk_masks(n):
    ri = lax.broadcasted_iota(I32, (n, n), 0)
    ci = lax.broadcasted_iota(I32, (n, n), 1)
    same = (ri // CHUNK) == (ci // CHUNK)
    return same, same & (ci <= ri)


def _gla_kernel(qk_ref, v_ref, r_ref, a_ref, gw_ref, gb_ref, ng_ref, o_ref, st_ref):
    @pl.when(pl.program_id(1) == 0)
    def _():
        st_ref[...] = jnp.zeros_like(st_ref)

    dk = GLA_DK // GLA_HEADS
    dv = GLA_DV // GLA_HEADS
    L = CHUNK
    same, causal = _chunk_masks(SUB)
    tri = jnp.where(causal, 1.0, 0.0).astype(BF16)
    blk = jnp.where(same, 1.0, 0.0).astype(BF16)
    lane = lax.broadcasted_iota(I32, (SUB, LANES), 1)
    gw = gw_ref[...]
    gb = gb_ref[...]
    ng = ng_ref[...]
    nsub = SCAN_ROWS // SUB
    subs = [slice(i * SUB, (i + 1) * SUB) for i in range(nsub)]
    heads = range(GLA_HEADS)
    chunks = [slice(c * L, (c + 1) * L) for c in range(SUB // L)]
    pair = lambda h: slice((h // 2) * LANES, (h // 2 + 1) * LANES)

    log_a = [_log_sigmoid(_dot_f32(a_ref[rows, :], gw) + gb) / GLA_TAU for rows in subs]
    cum = [_dot_exact_lhs(tri, x) for x in log_a]
    tot = [_dot_exact_lhs(blk, x) for x in log_a]
    q_dec, k_inv, k_end, dec, v = [], [], [], [], []
    for i, rows in enumerate(subs):
        qk = qk_ref[rows, :]
        k = qk[:, GLA_DK:]
        q_dec.append((qk[:, :GLA_DK] * (dk ** -0.5) * jnp.exp(cum[i])).astype(BF16))
        k_inv.append((k * jnp.exp(-cum[i])).astype(BF16))
        k_end.append((k * jnp.exp(tot[i] - cum[i])).astype(BF16))
        dec.append(jnp.exp(tot[i]))
        v.append(v_ref[rows, :].astype(BF16))

    def q_masked(i, h):
        lo_lane = (h % 2) * dk
        return jnp.where((lane >= lo_lane) & (lane < lo_lane + dk), q_dec[i][:, pair(h)],
                         jnp.zeros((SUB, LANES), BF16))

    qm = [[q_masked(i, h) for h in heads] for i in range(nsub)]
    vh = [[v[i][:, h * dv:(h + 1) * dv] for h in heads] for i in range(nsub)]
    att = [[_dot(qm[i][h], k_inv[i][:, pair(h)], NT) for h in heads] for i in range(nsub)]
    kv_t = [[[_dot(vh[i][h][cr], k_end[i][cr, pair(h)], TN) for cr in chunks] for h in heads]
            for i in range(nsub)]
    intra = [[_dot(jnp.where(causal, att[i][h], 0.0).astype(BF16), vh[i][h]) for h in heads]
             for i in range(nsub)]

    prev = [[[None] * len(chunks) for _ in heads] for _ in range(nsub)]
    for h in heads:
        state = st_ref[h]
        for i in range(nsub):
            for c in range(len(chunks)):
                prev[i][h][c] = state.astype(BF16)
                state = state * dec[i][c * L:c * L + 1, pair(h)] + kv_t[i][h][c]
        st_ref[h] = state

    for i, rows in enumerate(subs):
        outs = []
        for h in heads:
            inter = [_dot(qm[i][h][cr], prev[i][h][c], NT) for c, cr in enumerate(chunks)]
            o = intra[i][h] + jnp.concatenate(inter, axis=0)
            outs.append(o * lax.rsqrt(jnp.mean(o * o, axis=-1, keepdims=True) + EPS))
        r = r_ref[rows, :]
        o_ref[rows, :] = (jnp.concatenate(outs, axis=1) * ng * (r * _sigmoid(r))).astype(o_ref.dtype)


def gla_scan(z, gate_w_pad, gate_b, norm_g, b, s):
    t = z.shape[0]
    nc = s // SCAN_ROWS
    wide = lambda col: pl.BlockSpec((SCAN_ROWS, MIX), lambda bi, c: (bi * nc + c, col))
    dk = GLA_DK // GLA_HEADS
    dv = GLA_DV // GLA_HEADS
    return pl.pallas_call(
        _gla_kernel,
        grid=(b, nc),
        in_specs=[wide(0), wide(1), wide(2),
                  pl.BlockSpec((SCAN_ROWS, LANES), lambda bi, c: (bi * nc + c, 3 * MIX // LANES)),
                  _resident((LANES, GLA_DK)), _resident((1, GLA_DK)), _resident((1, GLA_DV))],
        out_specs=pl.BlockSpec((SCAN_ROWS, MIX), lambda bi, c: (bi * nc + c, 0)),
        out_shape=jax.ShapeDtypeStruct((t, MIX), BF16),
        scratch_shapes=[pltpu.VMEM((GLA_HEADS, dv, LANES), F32)],
        compiler_params=_params("parallel", "arbitrary"),
        name="gla_scan",
    )(z, z, z, z, gate_w_pad, gate_b.reshape(1, GLA_DK), norm_g.reshape(1, GLA_DV))


TAIL = 8


def _mlstm_kernel(qk_ref, v_ref, og_ref, ifc_ref, ifr_ref, cw_ref, cb_ref, o_ref,
                  tail_ref, ct_ref, n_ref, m_ref, qkc_ref):
    @pl.when(pl.program_id(1) == 0)
    def _():
        tail_ref[...] = jnp.zeros_like(tail_ref)
        ct_ref[...] = jnp.zeros_like(ct_ref)
        n_ref[...] = jnp.zeros_like(n_ref)
        m_ref[...] = jnp.zeros_like(m_ref)

    nh = MLSTM_HEADS
    dk = MLSTM_DQK // nh
    dv = MLSTM_DV // nh
    L = CHUNK
    rows_n = qk_ref.shape[0]
    neg = -jnp.inf

    x = qk_ref[...]
    xx = jnp.concatenate([tail_ref[...], x], axis=0)
    cw = cw_ref[...]
    y = cb_ref[...] + sum(
        cw[w:w + 1, :] * xx[TAIL - (MLSTM_CONV - 1) + w:TAIL - (MLSTM_CONV - 1) + w + rows_n, :]
        for w in range(MLSTM_CONV))
    tail_ref[...] = x[rows_n - TAIL:, :]
    qkc_ref[...] = y * _sigmoid(y)

    same, causal = _chunk_masks(SUB)
    tri = jnp.where(causal, 1.0, 0.0).astype(BF16)
    blk = jnp.where(same, 1.0, 0.0).astype(BF16)
    ri = lax.broadcasted_iota(I32, (SUB, SUB), 0)
    ci = lax.broadcasted_iota(I32, (SUB, SUB), 1)
    tri_t = jnp.where(same & (ri <= ci), 1.0, 0.0).astype(BF16)

    nsub = rows_n // SUB
    subs = [slice(i * SUB, (i + 1) * SUB) for i in range(nsub)]
    heads = range(nh)
    chunks = [slice(c * L, (c + 1) * L) for c in range(SUB // L)]
    nck = len(chunks)
    hcol = lambda h: slice(h * dk, (h + 1) * dk)

    ifc = [ifc_ref[rows, :] for rows in subs]
    ifr = [ifr_ref[i] for i in range(nsub)]
    lf_c = [_log_sigmoid(t) for t in ifc]
    cum_c = [_dot_exact_lhs(tri, t) for t in lf_c]
    tot_c = [_dot_exact_lhs(blk, t) for t in lf_c]
    cum_r = [sum(_dot(t, tri_t) for t in _split3(_log_sigmoid(u))) for u in ifr]

    a_end = [[tot_c[i][:, nh + h:nh + h + 1] - cum_c[i][:, nh + h:nh + h + 1] + ifc[i][:, h:h + 1]
              for h in heads] for i in range(nsub)]
    m_prev = [[[None] * nck for _ in heads] for _ in range(nsub)]
    m_next = [[[None] * nck for _ in heads] for _ in range(nsub)]
    for h in heads:
        m_run = m_ref[h:h + 1, 0:1]
        for i in range(nsub):
            for c, cr in enumerate(chunks):
                tot = tot_c[i][c * L:c * L + 1, nh + h:nh + h + 1]
                m_prev[i][h][c] = m_run
                m_run = jnp.maximum(tot + m_run, jnp.max(a_end[i][h][cr], axis=0, keepdims=True))
                m_next[i][h][c] = m_run
        m_ref[h:h + 1, :] = jnp.broadcast_to(m_run, (1, LANES))

    def per_row(vals):
        return jnp.concatenate([jnp.broadcast_to(t, (L, 1)) for t in vals], axis=0)

    qf = [[qkc_ref[rows, hcol(h)] for h in heads] for rows in subs]
    kf = [[qkc_ref[rows, MLSTM_DQK + h * dk:MLSTM_DQK + (h + 1) * dk] * (dk ** -0.5) for h in heads]
          for rows in subs]
    q = [[t.astype(BF16) for t in r] for r in qf]
    k = [[t.astype(BF16) for t in r] for r in kf]
    v = [[v_ref[rows, h * dv:(h + 1) * dv].astype(BF16) for h in heads] for rows in subs]
    qk = [[_dot(q[i][h], k[i][h], NT) for h in heads] for i in range(nsub)]

    kv_t = [[[None] * nck for _ in heads] for _ in range(nsub)]
    n_add = [[[None] * nck for _ in heads] for _ in range(nsub)]
    dec = [[[None] * nck for _ in heads] for _ in range(nsub)]
    for i in range(nsub):
        for h in heads:
            w = jnp.exp(a_end[i][h] - per_row(m_next[i][h]))
            wk = w * kf[i][h]
            for c, cr in enumerate(chunks):
                tot = tot_c[i][c * L:c * L + 1, nh + h:nh + h + 1]
                dec[i][h][c] = jnp.exp(tot + m_prev[i][h][c] - m_next[i][h][c])
                kv_t[i][h][c] = _dot(v[i][h][cr], wk[cr].astype(BF16), TN)
                n_add[i][h][c] = jnp.sum(wk[cr], axis=0, keepdims=True)

    w_inter = [[None] * nh for _ in range(nsub)]
    m_row = [[None] * nh for _ in range(nsub)]
    w_intra = [[None] * nh for _ in range(nsub)]
    for i in range(nsub):
        for h in heads:
            cum_i = cum_c[i][:, nh + h:nh + h + 1]
            log_d = jnp.where(causal, cum_i - cum_r[i][nh + h:nh + h + 1, :] + ifr[i][h:h + 1, :], neg)
            log_inter = cum_i + per_row(m_prev[i][h])
            m_i = jnp.maximum(log_inter, jnp.max(log_d, axis=1, keepdims=True))
            w_intra[i][h] = qk[i][h] * jnp.exp(log_d - m_i)
            w_inter[i][h] = jnp.exp(log_inter - m_i)
            m_row[i][h] = m_i
    num_intra = [[_dot(w_intra[i][h].astype(BF16), v[i][h]) for h in heads] for i in range(nsub)]

    c_prev = [[[None] * nck for _ in heads] for _ in range(nsub)]
    n_prev = [[[None] * nck for _ in heads] for _ in range(nsub)]
    for h in heads:
        ct = ct_ref[h]
        nv = n_ref[h:h + 1, :]
        for i in range(nsub):
            for c in range(nck):
                c_prev[i][h][c] = ct.astype(BF16)
                n_prev[i][h][c] = nv
                ct = dec[i][h][c] * ct + kv_t[i][h][c]
                nv = dec[i][h][c] * nv + n_add[i][h][c]
        ct_ref[h] = ct
        n_ref[h:h + 1, :] = nv

    for i, rows in enumerate(subs):
        outs = []
        for h in heads:
            inter = jnp.concatenate([_dot(q[i][h][cr], c_prev[i][h][c], NT)
                                     for c, cr in enumerate(chunks)], axis=0)
            qn = jnp.concatenate([jnp.sum(qf[i][h][cr] * n_prev[i][h][c], axis=1, keepdims=True)
                                  for c, cr in enumerate(chunks)], axis=0)
            num = w_inter[i][h] * inter + num_intra[i][h]
            den = w_inter[i][h] * qn + jnp.sum(w_intra[i][h], axis=1, keepdims=True)
            outs.append(num / jnp.maximum(jnp.abs(den), jnp.exp(-m_row[i][h])))
        og = og_ref[rows, :]
        o_ref[rows, :] = (jnp.concatenate(outs, axis=1) * _sigmoid(og)).astype(o_ref.dtype)


def mlstm_scan(z, ifr, conv_w, conv_b, b, s):
    t = z.shape[0]
    nc = s // SCAN_ROWS
    cpt = SCAN_ROWS // SUB
    dk = MLSTM_DQK // MLSTM_HEADS
    dv = MLSTM_DV // MLSTM_HEADS
    row = lambda w, col: pl.BlockSpec((SCAN_ROWS, w), lambda bi, c: (bi * nc + c, col))
    return pl.pallas_call(
        _mlstm_kernel,
        grid=(b, nc),
        in_specs=[row(2 * MLSTM_DQK, 0), row(MIX, 2), row(MIX, 3),
                  row(LANES, 4 * MIX // LANES),
                  pl.BlockSpec((cpt, 2 * MLSTM_HEADS, SUB), lambda bi, c: (bi * nc + c, 0, 0)),
                  _resident((MLSTM_CONV, 2 * MLSTM_DQK)), _resident((1, 2 * MLSTM_DQK))],
        out_specs=row(MIX, 0),
        out_shape=jax.ShapeDtypeStruct((t, MIX), BF16),
        scratch_shapes=[pltpu.VMEM((TAIL, 2 * MLSTM_DQK), F32),
                        pltpu.VMEM((MLSTM_HEADS, dv, dk), F32),
                        pltpu.VMEM((8, dk), F32),
                        pltpu.VMEM((8, LANES), F32),
                        pltpu.VMEM((SCAN_ROWS, 2 * MLSTM_DQK), F32)],
        compiler_params=_params("parallel", "arbitrary"),
        name="mlstm_scan",
    )(z, z, z, z, ifr, conv_w, conv_b.reshape(1, -1))


def _merge_kernel(x_ref, g_ref, wg_ref, bg_ref, ya_ref, yb_ref, yc_ref,
                  wa_ref, wb_ref, wc_ref, wo_ref, o_ref):
    x = x_ref[...]
    h = _rms(x, g_ref[...]).astype(BF16)
    d = D_MODEL
    merged = None
    for i, (y_ref, w_ref) in enumerate(((ya_ref, wa_ref), (yb_ref, wb_ref), (yc_ref, wc_ref))):
        gate = _sigmoid(_dot(h, wg_ref[:, i * d:(i + 1) * d]) + bg_ref[:, i * d:(i + 1) * d])
        term = gate * _dot(y_ref[...], w_ref[...])
        merged = term if merged is None else merged + term
    o_ref[...] = x + _dot(merged.astype(BF16), wo_ref[...])


def merge_out(x2d, g, w_gate, b_gate, ya, yb, yc, wa, wb, wc, wo, *, tm):
    t, d = x2d.shape
    row = lambda w: pl.BlockSpec((tm, w), lambda i: (i, 0))
    return pl.pallas_call(
        _merge_kernel,
        grid=(t // tm,),
        in_specs=[row(d), _resident((1, d)), _resident((d, N_BRANCHES * d)),
                  _resident((1, N_BRANCHES * d)), row(MIX), row(MIX), row(MIX),
                  _resident((MIX, d)), _resident((MIX, d)), _resident((MIX, d)),
                  _resident((d, d))],
        out_specs=row(d),
        out_shape=jax.ShapeDtypeStruct((t, d), F32),
        compiler_params=_params("parallel"),
        name="merge_out",
    )(x2d, g.reshape(1, d), w_gate, b_gate.reshape(1, -1), ya, yb, yc, wa, wb, wc, wo)


def _ffn_kernel(x_ref, g_ref, wg_ref, wu_ref, wd_ref, o_ref, h_ref, acc_ref):
    f = pl.program_id(1)

    @pl.when(f == 0)
    def _():
        h_ref[...] = _rms(x_ref[...], g_ref[...]).astype(BF16)
        acc_ref[...] = jnp.zeros_like(acc_ref)

    h = h_ref[...]
    gt = _dot(h, wg_ref[...])
    up = _dot(h, wu_ref[...])
    act = (gt * _sigmoid(gt) * up).astype(BF16)
    acc_ref[...] += _dot(act, wd_ref[...])

    @pl.when(f == pl.num_programs(1) - 1)
    def _():
        o_ref[...] = x_ref[...] + acc_ref[...]


def _ff_tile(ff):
    for cand in (1408, 1024, 768, 512, 256, 128):
        if ff % cand == 0:
            return cand
    raise ValueError(f"unsupported d_ff {ff}")


def dense_ffn(x2d, g, wg, wu, wd, *, tm):
    t, d = x2d.shape
    ff = wg.shape[1]
    tf = _ff_tile(ff)
    return pl.pallas_call(
        _ffn_kernel,
        grid=(t // tm, ff // tf),
        in_specs=[pl.BlockSpec((tm, d), lambda i, f: (i, 0)), _resident((1, d)),
                  pl.BlockSpec((d, tf), lambda i, f: (0, f)),
                  pl.BlockSpec((d, tf), lambda i, f: (0, f)),
                  pl.BlockSpec((tf, d), lambda i, f: (f, 0))],
        out_specs=pl.BlockSpec((tm, d), lambda i, f: (i, 0)),
        out_shape=jax.ShapeDtypeStruct((t, d), F32),
        scratch_shapes=[pltpu.VMEM((tm, d), BF16), pltpu.VMEM((tm, d), F32)],
        compiler_params=_params("parallel", "arbitrary"),
        name="dense_ffn",
    )(x2d, g.reshape(1, d), wg, wu, wd)


ROUTER_TM = 512
MOE_ROWS = 512
META_E, META_G, META_RANK = 0, 2, 4


def _router_kernel(x_ref, g_ref, rw_ref, rb_ref, h_ref, meta_ref, cnt_ref, carry_ref):
    @pl.when(pl.program_id(0) == 0)
    def _():
        carry_ref[...] = jnp.zeros_like(carry_ref)

    h = _rms(x_ref[...], g_ref[...])
    h_ref[...] = h
    tm = h.shape[0]
    lane = lax.broadcasted_iota(I32, (tm, LANES), 1)
    neg = -jnp.inf
    logits = jnp.where(lane < N_EXPERTS, _dot_f32(h, rw_ref[...]) + rb_ref[...], neg)
    m1 = jnp.max(logits, axis=1, keepdims=True)
    e1 = jnp.min(jnp.where(logits == m1, lane, LANES), axis=1, keepdims=True)
    rest = jnp.where(lane == e1, neg, logits)
    m2 = jnp.max(rest, axis=1, keepdims=True)
    e2 = jnp.min(jnp.where(rest == m2, lane, LANES), axis=1, keepdims=True)
    ex = jnp.exp(m2 - m1)
    g1 = 1.0 / (1.0 + ex)
    g2 = ex / (1.0 + ex)
    oh1 = jnp.where(lane == e1, 1.0, 0.0)
    oh2 = jnp.where(lane == e2, 1.0, 0.0)
    cnt = oh1 + oh2
    ri = lax.broadcasted_iota(I32, (tm, tm), 0)
    ci = lax.broadcasted_iota(I32, (tm, tm), 1)
    strict = jnp.where(ci < ri, 1.0, 0.0).astype(BF16)
    excl = _dot(strict, cnt.astype(BF16)) + carry_ref[...]
    r1 = jnp.sum(excl * oh1, axis=1, keepdims=True)
    r2 = jnp.sum(excl * oh2, axis=1, keepdims=True)
    carry_ref[...] += jnp.sum(cnt, axis=0, keepdims=True)
    cnt_ref[...] = carry_ref[...]
    meta = jnp.zeros((tm, LANES), F32)
    for pos, val in ((META_E, e1.astype(F32)), (META_E + 1, e2.astype(F32)),
                     (META_G, g1), (META_G + 1, g2), (META_RANK, r1), (META_RANK + 1, r2)):
        meta = jnp.where(lane == pos, val, meta)
    meta_ref[...] = meta


def moe_router(x2d, g, rw_pad, rb_pad):
    t, d = x2d.shape
    tm = ROUTER_TM
    return pl.pallas_call(
        _router_kernel,
        grid=(t // tm,),
        in_specs=[pl.BlockSpec((tm, d), lambda i: (i, 0)), _resident((1, d)),
                  _resident((d, LANES)), _resident((1, LANES))],
        out_specs=[pl.BlockSpec((tm, d), lambda i: (i, 0)),
                   pl.BlockSpec((tm, LANES), lambda i: (i, 0)),
                   pl.BlockSpec((1, LANES), lambda i: (0, 0))],
        out_shape=[jax.ShapeDtypeStruct((t, d), F32), jax.ShapeDtypeStruct((t, LANES), F32),
                   jax.ShapeDtypeStruct((1, LANES), F32)],
        scratch_shapes=[pltpu.VMEM((1, LANES), F32)],
        compiler_params=_params("arbitrary"),
        name="moe_router",
    )(x2d, g.reshape(1, d), rw_pad, rb_pad)


DISPATCH_TM = 1024
ISSUE_UNROLL = 8


def _row_copy(src_ref, dst_ref, src_row, dst_row, sem):
    return pltpu.make_async_copy(src_ref.at[pl.ds(src_row, 1)], dst_ref.at[pl.ds(dst_row, 1)], sem)


def _dispatch_kernel(d1_ref, d2_ref, h_ref, init_hbm, xs_hbm, sem):
    del init_hbm
    n = h_ref.shape[0]
    per_trip = ISSUE_UNROLL // TOP_K

    def start(i, _):
        for u in range(per_trip):
            r = i * per_trip + u
            _row_copy(h_ref, xs_hbm, r, d1_ref[0, 0, r], sem).start(priority=0)
            _row_copy(h_ref, xs_hbm, r, d2_ref[0, 0, r], sem).start(priority=1)
        return 0

    def wait(i, _):
        for u in range(ISSUE_UNROLL):
            _row_copy(h_ref, xs_hbm, 0, 0, sem).wait()
        return 0

    lax.fori_loop(0, n // per_trip, start, 0)
    lax.fori_loop(0, n // per_trip, wait, 0)


def dispatch_rows(h, dest1, dest2, n_rows):
    t, d = h.shape
    tm = DISPATCH_TM
    nblk = t // tm
    idx = pl.BlockSpec((1, 1, tm), lambda i: (i, 0, 0), memory_space=pltpu.SMEM)
    return pl.pallas_call(
        _dispatch_kernel,
        grid=(nblk,),
        in_specs=[idx, idx, pl.BlockSpec((tm, d), lambda i: (i, 0)),
                  pl.BlockSpec(memory_space=pl.ANY)],
        out_specs=pl.BlockSpec(memory_space=pl.ANY),
        out_shape=jax.ShapeDtypeStruct((n_rows, d), h.dtype),
        input_output_aliases={3: 0},
        scratch_shapes=[pltpu.SemaphoreType.DMA(())],
        compiler_params=_params("arbitrary"),
        name="dispatch_rows",
    )(dest1.reshape(nblk, 1, tm), dest2.reshape(nblk, 1, tm), h, jnp.zeros((n_rows, d), h.dtype))


def _grouped_ffn_kernel(be_ref, bv_ref, x_ref, wg_ref, wu_ref, wd_ref, o_ref, acc_ref):
    i = pl.program_id(0)
    f = pl.program_id(1)

    @pl.when(f == 0)
    def _():
        acc_ref[...] = jnp.zeros_like(acc_ref)

    @pl.when(bv_ref[i] > 0)
    def _():
        h = x_ref[...].astype(BF16)
        gt = _dot(h, wg_ref[...])
        up = _dot(h, wu_ref[...])
        act = (gt * _sigmoid(gt) * up).astype(BF16)
        acc_ref[...] += _dot(act, wd_ref[...])

    @pl.when(f == pl.num_programs(1) - 1)
    def _():
        o_ref[...] = acc_ref[...]


def grouped_ffn(xs, blk_e, blk_valid, wg, wu, wd):
    n, d = xs.shape
    ff = wg.shape[2]
    tf = _ff_tile(ff)
    nblk = n // MOE_ROWS
    grid_spec = pltpu.PrefetchScalarGridSpec(
        num_scalar_prefetch=2,
        grid=(nblk, ff // tf),
        in_specs=[pl.BlockSpec((MOE_ROWS, d), lambda i, f, be, bv: (i, 0)),
                  pl.BlockSpec((None, d, tf), lambda i, f, be, bv: (be[i], 0, f)),
                  pl.BlockSpec((None, d, tf), lambda i, f, be, bv: (be[i], 0, f)),
                  pl.BlockSpec((None, tf, d), lambda i, f, be, bv: (be[i], f, 0))],
        out_specs=pl.BlockSpec((MOE_ROWS, d), lambda i, f, be, bv: (i, 0)),
        scratch_shapes=[pltpu.VMEM((MOE_ROWS, d), F32)],
    )
    return pl.pallas_call(
        _grouped_ffn_kernel,
        grid_spec=grid_spec,
        out_shape=jax.ShapeDtypeStruct((n, d), F32),
        compiler_params=_params("parallel", "arbitrary"),
        name="grouped_ffn",
    )(blk_e, blk_valid, xs, wg, wu, wd)


COMBINE_TM = 1024


def _combine_kernel(d1_ref, d2_ref, x_ref, meta_ref, fg_ref, ys_hbm, o_ref, b1_ref, b2_ref, sem):
    n = x_ref.shape[0]

    def start(i, _):
        for u in range(ISSUE_UNROLL // 2):
            r = i * (ISSUE_UNROLL // 2) + u
            _row_copy(ys_hbm, b1_ref, d1_ref[0, 0, r], r, sem.at[0]).start(priority=0)
            _row_copy(ys_hbm, b2_ref, d2_ref[0, 0, r], r, sem.at[1]).start(priority=1)
        return 0

    def wait(i, _):
        for u in range(ISSUE_UNROLL // 2):
            r = i * (ISSUE_UNROLL // 2) + u
            _row_copy(ys_hbm, b1_ref, 0, r, sem.at[0]).wait()
            _row_copy(ys_hbm, b2_ref, 0, r, sem.at[1]).wait()
        return 0

    lax.fori_loop(0, n // (ISSUE_UNROLL // 2), start, 0)
    lax.fori_loop(0, n // (ISSUE_UNROLL // 2), wait, 0)
    meta = meta_ref[...]
    g1 = meta[:, META_G:META_G + 1]
    g2 = meta[:, META_G + 1:META_G + 2]
    y = x_ref[...] + (g1 * b1_ref[...] + g2 * b2_ref[...])
    o_ref[...] = _rms(y, fg_ref[...])


def moe_combine_norm(x2d, meta, dest1, dest2, ys, final_g):
    t, d = x2d.shape
    tm = COMBINE_TM
    nblk = t // tm
    idx = pl.BlockSpec((1, 1, tm), lambda i: (i, 0, 0), memory_space=pltpu.SMEM)
    return pl.pallas_call(
        _combine_kernel,
        grid=(nblk,),
        in_specs=[idx, idx, pl.BlockSpec((tm, d), lambda i: (i, 0)),
                  pl.BlockSpec((tm, LANES), lambda i: (i, 0)), _resident((1, d)),
                  pl.BlockSpec(memory_space=pl.ANY)],
        out_specs=pl.BlockSpec((tm, d), lambda i: (i, 0)),
        out_shape=jax.ShapeDtypeStruct((t, d), F32),
        scratch_shapes=[pltpu.VMEM((tm, d), F32), pltpu.VMEM((tm, d), F32),
                        pltpu.SemaphoreType.DMA((2,))],
        compiler_params=_params("arbitrary"),
        name="moe_combine_norm",
    )(dest1.reshape(nblk, 1, tm), dest2.reshape(nblk, 1, tm), x2d, meta,
      final_g.reshape(1, d), ys)


def moe_layer_final(x2d, norm_g, router_w, router_b, wg, wu, wd, final_g):
    t, d = x2d.shape
    rw_pad = jnp.pad(router_w, ((0, 0), (0, LANES - N_EXPERTS)))
    rb_pad = jnp.pad(router_b, (0, LANES - N_EXPERTS)).reshape(1, LANES)
    h, meta, counts = moe_router(x2d, norm_g, rw_pad, rb_pad)

    counts = counts[0, :N_EXPERTS].astype(I32)
    padded = (counts + MOE_ROWS - 1) // MOE_ROWS * MOE_ROWS
    pad_ends = jnp.cumsum(padded)
    pad_starts = pad_ends - padded
    e = meta[:, META_E:META_E + TOP_K].astype(I32)
    rank = meta[:, META_RANK:META_RANK + TOP_K].astype(I32)
    dest = pad_starts[e] + rank
    n_blk = (t * TOP_K) // MOE_ROWS + N_EXPERTS
    n_rows = n_blk * MOE_ROWS
    blk_start = jnp.arange(n_blk, dtype=I32) * MOE_ROWS
    blk_valid = (blk_start < pad_ends[-1]).astype(I32)
    blk_e = jnp.minimum(jnp.sum((blk_start[:, None] >= pad_ends[None, :]).astype(I32), axis=1),
                        N_EXPERTS - 1)

    xs = dispatch_rows(h, dest[:, 0], dest[:, 1], n_rows)
    ys = grouped_ffn(xs, blk_e, blk_valid, wg, wu, wd)
    return moe_combine_norm(x2d, meta, dest[:, 0], dest[:, 1], ys, final_g)


def _col_slices():
    sizes = (('moba', 3 * MIX), ('gla_qkv', 2 * GLA_DK + GLA_DV), ('gla_a', GLA_GATE_RANK),
             ('gla_r', GLA_DV), ('mlstm_qkv', 2 * MLSTM_DQK + MLSTM_DV),
             ('mlstm_if', 2 * MLSTM_HEADS), ('mlstm_o', MLSTM_DV),
             ('merge_gate', N_BRANCHES * D_MODEL))
    out, off = {}, 0
    for name, size in sizes:
        out[name] = slice(off, off + size)
        off += size
    return out


def _pad_cols(a, width):
    return jnp.pad(a, ((0, 0), (0, width - a.shape[1])))


def hybrid_mixer_layer(x2d, b, s, norm_g, w_in, b_in, conv_w, conv_b, gate_w, gate_b, gla_norm_g,
                       w_up_moba, w_up_gla, w_up_mlstm, w_o, tables):
    cs = _col_slices()
    bias = b_in.reshape(1, -1)
    wcat = lambda names: jnp.concatenate([w_in[:, cs[n]] for n in names], axis=1)
    bcat = lambda names: jnp.concatenate([bias[:, cs[n]] for n in names], axis=1)

    nb = s // MOBA_BLOCK
    q, k, vt, km = moba_proj(x2d, norm_g, w_in[:, cs['moba']].astype(BF16), bias[:, cs['moba']],
                             tables, s, tm=512)
    y_moba = moba_attention(q.reshape(b, s, MIX), k.reshape(b, s, MIX),
                            vt.reshape(b, nb, MIX, MOBA_BLOCK),
                            km.reshape(b, nb, MIX)).reshape(b * s, MIX)

    w_gla = jnp.concatenate([wcat(['gla_qkv', 'gla_r']), _pad_cols(w_in[:, cs['gla_a']], LANES)], axis=1)
    b_gla = jnp.concatenate([bcat(['gla_qkv', 'gla_r']), _pad_cols(bias[:, cs['gla_a']], LANES)], axis=1)
    z_gla = norm_matmul(x2d, norm_g, w_gla.astype(BF16), b_gla, tm=512)
    gate_w_pad = jnp.pad(gate_w, ((0, LANES - GLA_GATE_RANK), (0, 0)))
    y_gla = gla_scan(z_gla, gate_w_pad, gate_b, gla_norm_g, b, s)

    w_ml = jnp.concatenate([wcat(['mlstm_qkv', 'mlstm_o']), _pad_cols(w_in[:, cs['mlstm_if']], LANES)], axis=1)
    b_ml = jnp.concatenate([bcat(['mlstm_qkv', 'mlstm_o']), _pad_cols(bias[:, cs['mlstm_if']], LANES)], axis=1)
    z_ml = norm_matmul(x2d, norm_g, w_ml.astype(BF16), b_ml, tm=512)
    n_if = 2 * MLSTM_HEADS
    ifr = z_ml[:, 4 * MIX:4 * MIX + n_if].reshape(b * s // SUB, SUB, n_if).transpose(0, 2, 1)
    y_ml = mlstm_scan(z_ml, ifr, conv_w, conv_b, b, s)

    return merge_out(x2d, norm_g, w_in[:, cs['merge_gate']].astype(BF16), bias[:, cs['merge_gate']],
                     y_moba, y_gla, y_ml, w_up_moba.astype(BF16), w_up_gla.astype(BF16),
                     w_up_mlstm.astype(BF16), w_o.astype(BF16), tm=512)


def kernel(x, attn_norm_g, w_in, b_in, mlstm_conv_w, mlstm_conv_b, gla_gate_w, gla_gate_b, gla_norm_g, w_up_moba, w_up_gla, w_up_mlstm, w_o, ffn_norm_g, ffn_w_gate, ffn_w_up, ffn_w_down, router_w, router_b, moe_w_gate, moe_w_up, moe_w_down, final_norm_g):
    b, s, d = x.shape
    depth = w_in.shape[0]
    assert d == D_MODEL and depth == 2, "two layers: dense SwiGLU then MoE"
    assert s % SCAN_ROWS == 0 and s % MOBA_BLOCK == 0
    x2d = x.reshape(b * s, d)
    tables = _rope_tables(s)
    for l in range(depth):
        x2d = hybrid_mixer_layer(x2d, b, s, attn_norm_g[l], w_in[l], b_in[l], mlstm_conv_w[l],
                                 mlstm_conv_b[l], gla_gate_w[l], gla_gate_b[l], gla_norm_g[l],
                                 w_up_moba[l], w_up_gla[l], w_up_mlstm[l], w_o[l], tables)
        j = l // 2
        if l % 2 == 0:
            x2d = dense_ffn(x2d, ffn_norm_g[l], ffn_w_gate[j].astype(BF16), ffn_w_up[j].astype(BF16),
                            ffn_w_down[j].astype(BF16), tm=512)
        else:
            x2d = moe_layer_final(x2d, ffn_norm_g[l], router_w[j], router_b[j],
                                  moe_w_gate[j].astype(BF16), moe_w_up[j].astype(BF16),
                                  moe_w_down[j].astype(BF16), final_norm_g)
    return x2d.reshape(b, s, d)
```

```python
import functools

import jax
import jax.numpy as jnp
from jax import lax
from jax.experimental import pallas as pl
from jax.experimental.pallas import tpu as pltpu

F32 = jnp.float32
BF16 = jnp.bfloat16
I32 = jnp.int32

D_MODEL = 1024
MIX = D_MODEL // 2
MOBA_HEAD_DIM = 64
MOBA_BLOCK = 256
MOBA_TOPK = 3
ROPE_THETA = 500000.0
ROPE_DIMS = MOBA_HEAD_DIM // 4
GLA_HEADS = 4
GLA_DK = MIX // 2
GLA_DV = MIX
GLA_GATE_RANK = 16
GLA_TAU = 16.0
MLSTM_HEADS = 4
MLSTM_DQK = MIX
MLSTM_DV = MIX
MLSTM_CONV = 4
CHUNK = 64
N_BRANCHES = 3
N_EXPERTS = 8
TOP_K = 2
EPS = 1e-6
LOG2E = 1.4426950408889634

LANES = 128
VMEM_LIMIT = 56 * 1024 * 1024

NN = (((1,), (0,)), ((), ()))
NT = (((1,), (1,)), ((), ()))
TN = (((0,), (0,)), ((), ()))


def _dot(a, b, dims=NN):
    return lax.dot_general(a, b, dims, preferred_element_type=F32)


def _dot_f32(a, b, dims=NN):
    return lax.dot_general(a, b, dims, preferred_element_type=F32,
                           precision=lax.Precision.HIGHEST)


def _rms(x, g):
    return x * lax.rsqrt(jnp.mean(x * x, axis=-1, keepdims=True) + EPS) * g


def _sigmoid(x):
    return 1.0 / (1.0 + jnp.exp(-x))


def _log_sigmoid(x):
    return jnp.minimum(x, 0.0) - jnp.log(1.0 + jnp.exp(-jnp.abs(x)))


def _params(*sem):
    return pltpu.CompilerParams(dimension_semantics=sem, vmem_limit_bytes=VMEM_LIMIT)


def _resident(shape):
    n = len(shape)
    return pl.BlockSpec(shape, lambda *_: (0,) * n, pipeline_mode=pl.Buffered(1))


def _rope_tables(s):
    half = ROPE_DIMS // 2
    inv_freq = jnp.power(ROPE_THETA, -jnp.arange(half, dtype=F32) * 2.0 / ROPE_DIMS)
    ang = jnp.arange(s).astype(F32)[:, None] * inv_freq[None, :]
    cos, sin = jnp.cos(ang), jnp.sin(ang)
    ones = jnp.ones((s, MOBA_HEAD_DIM - ROPE_DIMS), F32)
    zeros = jnp.zeros((s, MOBA_HEAD_DIM - ROPE_DIMS), F32)
    zh = jnp.zeros((s, half), F32)
    c = jnp.concatenate([cos, cos, ones], axis=1)
    sa = jnp.concatenate([zh, sin, zeros], axis=1)
    sb = jnp.concatenate([-sin, zh, zeros], axis=1)
    rep = LANES // MOBA_HEAD_DIM
    return tuple(jnp.tile(a, (1, rep)) for a in (c, sa, sb))


def _mixer_proj_kernel(x_ref, g_ref, wa_ref, ba_ref, wb_ref, bb_ref, wc_ref, bc_ref,
                       c_ref, sa_ref, sb_ref,
                       q_ref, k_ref, vt_ref, km_ref, zb_ref, zc_ref):
    h = _rms(x_ref[...], g_ref[...]).astype(BF16)
    zb_ref[...] = _dot(h, wb_ref[...]) + bb_ref[...]
    zc_ref[...] = _dot(h, wc_ref[...]) + bc_ref[...]
    z = _dot(h, wa_ref[...]) + ba_ref[...]
    tm = z.shape[0]
    rep = MIX // LANES
    c = jnp.concatenate([c_ref[...]] * rep, axis=1)
    sa = jnp.concatenate([sa_ref[...]] * rep, axis=1)
    sb = jnp.concatenate([sb_ref[...]] * rep, axis=1)
    half = ROPE_DIMS // 2

    def rot(t):
        return t * c + pltpu.roll(t, half, 1) * sa + pltpu.roll(t, MIX - half, 1) * sb

    q = rot(z[:, :MIX]) * (MOBA_HEAD_DIM ** -0.5 * LOG2E)
    k = rot(z[:, MIX:2 * MIX])
    q_ref[...] = q.astype(BF16)
    k_ref[...] = k.astype(BF16)
    nb = tm // MOBA_BLOCK
    for i in range(nb):
        rows = slice(i * MOBA_BLOCK, (i + 1) * MOBA_BLOCK)
        vt_ref[i] = z[rows, 2 * MIX:].T.astype(BF16)
        km_ref[i] = jnp.mean(k[rows, :], axis=0, keepdims=True)


def mixer_proj(x2d, g, wa, ba, wb, bb, wc, bc, tables, s, *, tm):
    t, d = x2d.shape
    c, sa, sb = tables
    spt = s // tm
    tab = pl.BlockSpec((tm, LANES), lambda i: (i % spt, 0))
    row = lambda w: pl.BlockSpec((tm, w), lambda i: (i, 0))
    nb = tm // MOBA_BLOCK
    na, nbw, ncw = wa.shape[1], wb.shape[1], wc.shape[1]
    return pl.pallas_call(
        _mixer_proj_kernel,
        grid=(t // tm,),
        in_specs=[row(d), _resident((1, d)),
                  _resident((d, na)), _resident((1, na)),
                  _resident((d, nbw)), _resident((1, nbw)),
                  _resident((d, ncw)), _resident((1, ncw)), tab, tab, tab],
        out_specs=[row(MIX), row(MIX), pl.BlockSpec((nb, MIX, MOBA_BLOCK), lambda i: (i, 0, 0)),
                   pl.BlockSpec((nb, 1, MIX), lambda i: (i, 0, 0)), row(nbw), row(ncw)],
        out_shape=[jax.ShapeDtypeStruct((t, MIX), BF16)] * 2
        + [jax.ShapeDtypeStruct((t // MOBA_BLOCK, MIX, MOBA_BLOCK), BF16),
           jax.ShapeDtypeStruct((t // MOBA_BLOCK, 1, MIX), F32),
           jax.ShapeDtypeStruct((t, nbw), F32), jax.ShapeDtypeStruct((t, ncw), F32)],
        compiler_params=_params("parallel"),
        name="mixer_proj",
    )(x2d, g.reshape(1, d), wa, ba, wb, bb, wc, bc, c, sa, sb)


MOBA_TQ = MOBA_BLOCK
MOBA_LANES = 256
HEADS_PER_STEP = MOBA_LANES // MOBA_HEAD_DIM
SUM_ROWS = 16


def _moba_attn_kernel(q_ref, k_ref, vt_ref, km_ref, o_ref, bias_ref, qh_ref, s_ref, p_ref):
    cur = pl.program_id(2)
    nb = km_ref.shape[1]
    hd = MOBA_HEAD_DIM
    neg = -jnp.inf
    own = pl.multiple_of(cur * MOBA_BLOCK, MOBA_BLOCK)
    nh = HEADS_PER_STEP

    ones_rows = jnp.ones((SUM_ROWS, MOBA_BLOCK), BF16)

    def v_rows(blk, hh):
        return jnp.concatenate([vt_ref[0, blk, hh * hd:(hh + 1) * hd, :], ones_rows], axis=0)

    def scores_into(slot, blk):
        off = pl.multiple_of(blk * MOBA_BLOCK, MOBA_BLOCK)
        kb = k_ref[0, pl.ds(off, MOBA_BLOCK), :]
        for hh in range(nh):
            s_ref[slot, hh] = _dot(kb, qh_ref[hh], NT)

    q_all = q_ref[0]
    lane = lax.broadcasted_iota(I32, q_all.shape, 1)
    km = km_ref[0]
    km_hi = km.astype(BF16)
    km_lo = (km - km_hi.astype(F32)).astype(BF16)
    k_own = k_ref[0, pl.ds(own, MOBA_BLOCK), :]
    k_first = k_ref[0, 0:MOBA_BLOCK, :]
    init = []
    for hh in range(nh):
        q = jnp.where((lane >= hh * hd) & (lane < (hh + 1) * hd), q_all, jnp.zeros_like(q_all))
        qh_ref[hh] = q
        lhs = jnp.concatenate([km_hi, km_lo, k_own, k_first], axis=0)
        prod = _dot(lhs, q, NT)
        gate = prod[:nb] + prod[nb:2 * nb]
        blk = lax.broadcasted_iota(I32, gate.shape, 0)
        gate = jnp.where(blk < cur, gate, neg)
        sel = jnp.zeros(gate.shape, F32)
        for _ in range(MOBA_TOPK):
            m = jnp.max(gate, axis=0, keepdims=True)
            first = jnp.min(jnp.where(gate == m, blk, nb), axis=0, keepdims=True)
            first = jnp.where(m > neg, first, nb)
            pick = blk == first
            sel = jnp.where(pick, 1.0, sel)
            gate = jnp.where(pick, neg, gate)
        bias_ref[hh] = jnp.where(sel > 0.0, 0.0, neg)

        s = prod[2 * nb:2 * nb + MOBA_BLOCK]
        s_ref[0, hh] = prod[2 * nb + MOBA_BLOCK:]
        k_pos = lax.broadcasted_iota(I32, s.shape, 0)
        q_pos = lax.broadcasted_iota(I32, s.shape, 1)
        s = jnp.where(k_pos <= q_pos, s, neg)
        m0 = jnp.max(s, axis=0, keepdims=True)
        p = jnp.exp2(s - m0)
        p_ref[hh] = p.astype(BF16)
        init.append((m0, jnp.zeros((hd + SUM_ROWS, MOBA_TQ), F32), jnp.ones((1, MOBA_TQ), F32)))

    def step(j, slot, carry):
        pend = jnp.where(j == 0, cur, j - 1)
        pv = [_dot(v_rows(pend, hh), p_ref[hh]) for hh in range(nh)]
        out = []
        for hh in range(nh):
            m_i, acc, a_pend = carry[hh]
            bias = bias_ref[hh, pl.ds(j, 1), :]
            m_new = jnp.maximum(m_i, jnp.max(s_ref[slot, hh], axis=0, keepdims=True) + bias)
            alpha = jnp.exp2(m_i - m_new)
            acc_new = a_pend * acc + pv[hh]
            p_ref[hh] = jnp.exp2(s_ref[slot, hh] + (bias - m_new)).astype(BF16)
            out.append((m_new, acc_new, alpha))
        scores_into(1 - slot, jnp.minimum(j + 1, nb - 1))
        return tuple(out)

    def body(i, carry):
        return step(2 * i + 1, 1, step(2 * i, 0, carry))

    trips = (cur + 1) // 2
    final = lax.fori_loop(0, trips, body, tuple(init))
    pend = jnp.where(trips == 0, cur, 2 * trips - 1)
    outs = []
    for hh in range(nh):
        _, acc, a_pend = final[hh]
        acc = a_pend * acc + _dot(v_rows(pend, hh), p_ref[hh])
        outs.append((acc[:hd] / acc[hd:hd + 1]).T)
    o_ref[0] = jnp.concatenate(outs, axis=1).astype(o_ref.dtype)


def moba_attention(q, k, vt, km):
    b, s, _ = q.shape
    nb = km.shape[1]
    qspec = pl.BlockSpec((1, MOBA_TQ, MOBA_LANES), lambda bi, hp, qi: (bi, qi, hp))
    return pl.pallas_call(
        _moba_attn_kernel,
        grid=(b, MIX // MOBA_LANES, s // MOBA_TQ),
        in_specs=[qspec,
                  pl.BlockSpec((1, s, MOBA_LANES), lambda bi, hp, qi: (bi, 0, hp)),
                  pl.BlockSpec((1, nb, MOBA_LANES, MOBA_BLOCK), lambda bi, hp, qi: (bi, 0, hp, 0)),
                  pl.BlockSpec((1, nb, MOBA_LANES), lambda bi, hp, qi: (bi, 0, hp))],
        out_specs=qspec,
        out_shape=jax.ShapeDtypeStruct((b, s, MIX), BF16),
        scratch_shapes=[pltpu.VMEM((HEADS_PER_STEP, nb, MOBA_TQ), F32),
                        pltpu.VMEM((HEADS_PER_STEP, MOBA_TQ, MOBA_LANES), BF16),
                        pltpu.VMEM((2, HEADS_PER_STEP, MOBA_BLOCK, MOBA_TQ), F32),
                        pltpu.VMEM((HEADS_PER_STEP, MOBA_BLOCK, MOBA_TQ), BF16)],
        compiler_params=_params("parallel", "parallel", "arbitrary"),
        name="moba_attention",
    )(q, k, vt, km)


SCAN_ROWS = 512


SUB = 128


def _split3(x):
    hi = x.astype(BF16)
    r1 = x - hi.astype(F32)
    mid = r1.astype(BF16)
    lo = (r1 - mid.astype(F32)).astype(BF16)
    return hi, mid, lo


def _dot_exact_lhs(lhs_bf16, x):
    return sum(_dot(lhs_bf16, t) for t in _split3(x))


def _chunk_masks(n):
    ri = lax.broadcasted_iota(I32, (n, n), 0)
    ci = lax.broadcasted_iota(I32, (n, n), 1)
    same = (ri // CHUNK) == (ci // CHUNK)
    return same, same & (ci <= ri)


def _gla_kernel(qk_ref, v_ref, r_ref, a_ref, gw_ref, gb_ref, ng_ref, o_ref, st_ref):
    @pl.when(pl.program_id(1) == 0)
    def _():
        st_ref[...] = jnp.zeros_like(st_ref)

    dk = GLA_DK // GLA_HEADS
    dv = GLA_DV // GLA_HEADS
    L = CHUNK
    same, causal = _chunk_masks(SUB)
    tri = jnp.where(causal, 1.0, 0.0).astype(BF16)
    blk = jnp.where(same, 1.0, 0.0).astype(BF16)
    lane = lax.broadcasted_iota(I32, (SUB, LANES), 1)
    gw = gw_ref[...]
    gb = gb_ref[...]
    ng = ng_ref[...]
    nsub = SCAN_ROWS // SUB
    subs = [slice(i * SUB, (i + 1) * SUB) for i in range(nsub)]
    heads = range(GLA_HEADS)
    chunks = [slice(c * L, (c + 1) * L) for c in range(SUB // L)]
    pair = lambda h: slice((h // 2) * LANES, (h // 2 + 1) * LANES)

    log_a = [_log_sigmoid(_dot_f32(a_ref[rows, :], gw) + gb) / GLA_TAU for rows in subs]
    cum = [_dot_exact_lhs(tri, x) for x in log_a]
    tot = [_dot_exact_lhs(blk, x) for x in log_a]
    q_dec, k_inv, k_end, dec, v = [], [], [], [], []
    for i, rows in enumerate(subs):
        qk = qk_ref[rows, :]
        k = qk[:, GLA_DK:]
        q_dec.append((qk[:, :GLA_DK] * (dk ** -0.5) * jnp.exp(cum[i])).astype(BF16))
        k_inv.append((k * jnp.exp(-cum[i])).astype(BF16))
        k_end.append((k * jnp.exp(tot[i] - cum[i])).astype(BF16))
        dec.append(jnp.exp(tot[i]))
        v.append(v_ref[rows, :].astype(BF16))

    def q_masked(i, h):
        lo_lane = (h % 2) * dk
        return jnp.where((lane >= lo_lane) & (lane < lo_lane + dk), q_dec[i][:, pair(h)],
                         jnp.zeros((SUB, LANES), BF16))

    qm = [[q_masked(i, h) for h in heads] for i in range(nsub)]
    vh = [[v[i][:, h * dv:(h + 1) * dv] for h in heads] for i in range(nsub)]
    att = [[_dot(qm[i][h], k_inv[i][:, pair(h)], NT) for h in heads] for i in range(nsub)]
    kv_t = [[[_dot(vh[i][h][cr], k_end[i][cr, pair(h)], TN) for cr in chunks] for h in heads]
            for i in range(nsub)]
    intra = [[_dot(jnp.where(causal, att[i][h], 0.0).astype(BF16), vh[i][h]) for h in heads]
             for i in range(nsub)]

    prev = [[[None] * len(chunks) for _ in heads] for _ in range(nsub)]
    for h in heads:
        state = st_ref[h]
        for i in range(nsub):
            for c in range(len(chunks)):
                prev[i][h][c] = state.astype(BF16)
                state = state * dec[i][c * L:c * L + 1, pair(h)] + kv_t[i][h][c]
        st_ref[h] = state

    for i, rows in enumerate(subs):
        outs = []
        for h in heads:
            inter = [_dot(qm[i][h][cr], prev[i][h][c], NT) for c, cr in enumerate(chunks)]
            o = intra[i][h] + jnp.concatenate(inter, axis=0)
            outs.append(o * lax.rsqrt(jnp.mean(o * o, axis=-1, keepdims=True) + EPS))
        r = r_ref[rows, :]
        o_ref[rows, :] = (jnp.concatenate(outs, axis=1) * ng * (r * _sigmoid(r))).astype(o_ref.dtype)


def gla_scan(z, gate_w_pad, gate_b, norm_g, b, s):
    t = z.shape[0]
    nc = s // SCAN_ROWS
    wide = lambda col: pl.BlockSpec((SCAN_ROWS, MIX), lambda bi, c: (bi * nc + c, col))
    dk = GLA_DK // GLA_HEADS
    dv = GLA_DV // GLA_HEADS
    return pl.pallas_call(
        _gla_kernel,
        grid=(b, nc),
        in_specs=[wide(0), wide(1), wide(2),
                  pl.BlockSpec((SCAN_ROWS, LANES), lambda bi, c: (bi * nc + c, 3 * MIX // LANES)),
                  _resident((LANES, GLA_DK)), _resident((1, GLA_DK)), _resident((1, GLA_DV))],
        out_specs=pl.BlockSpec((SCAN_ROWS, MIX), lambda bi, c: (bi * nc + c, 0)),
        out_shape=jax.ShapeDtypeStruct((t, MIX), BF16),
        scratch_shapes=[pltpu.VMEM((GLA_HEADS, dv, LANES), F32)],
        compiler_params=_params("parallel", "arbitrary"),
        name="gla_scan",
    )(z, z, z, z, gate_w_pad, gate_b.reshape(1, GLA_DK), norm_g.reshape(1, GLA_DV))


TAIL = 8


def _mlstm_kernel(qk_ref, v_ref, og_ref, ifc_ref, ifr_ref, cw_ref, cb_ref, o_ref,
                  tail_ref, ct_ref, n_ref, m_ref, qkc_ref):
    @pl.when(pl.program_id(1) == 0)
    def _():
        tail_ref[...] = jnp.zeros_like(tail_ref)
        ct_ref[...] = jnp.zeros_like(ct_ref)
        n_ref[...] = jnp.zeros_like(n_ref)
        m_ref[...] = jnp.zeros_like(m_ref)

    nh = MLSTM_HEADS
    dk = MLSTM_DQK // nh
    dv = MLSTM_DV // nh
    L = CHUNK
    rows_n = qk_ref.shape[0]
    neg = -jnp.inf

    x = qk_ref[...]
    xx = jnp.concatenate([tail_ref[...], x], axis=0)
    cw = cw_ref[...]
    y = cb_ref[...] + sum(
        cw[w:w + 1, :] * xx[TAIL - (MLSTM_CONV - 1) + w:TAIL - (MLSTM_CONV - 1) + w + rows_n, :]
        for w in range(MLSTM_CONV))
    tail_ref[...] = x[rows_n - TAIL:, :]
    qkc_ref[...] = y * _sigmoid(y)

    same, causal = _chunk_masks(SUB)
    tri = jnp.where(causal, 1.0, 0.0).astype(BF16)
    blk = jnp.where(same, 1.0, 0.0).astype(BF16)
    ri = lax.broadcasted_iota(I32, (SUB, SUB), 0)
    ci = lax.broadcasted_iota(I32, (SUB, SUB), 1)
    tri_t = jnp.where(same & (ri <= ci), 1.0, 0.0).astype(BF16)

    nsub = rows_n // SUB
    subs = [slice(i * SUB, (i + 1) * SUB) for i in range(nsub)]
    heads = range(nh)
    chunks = [slice(c * L, (c + 1) * L) for c in range(SUB // L)]
    nck = len(chunks)
    hcol = lambda h: slice(h * dk, (h + 1) * dk)

    ifc = [ifc_ref[rows, :] for rows in subs]
    ifr = [ifr_ref[i] for i in range(nsub)]
    lf_c = [_log_sigmoid(t) for t in ifc]
    cum_c = [_dot_exact_lhs(tri, t) for t in lf_c]
    tot_c = [_dot_exact_lhs(blk, t) for t in lf_c]
    cum_r = [sum(_dot(t, tri_t) for t in _split3(_log_sigmoid(u))) for u in ifr]

    a_end = [[tot_c[i][:, nh + h:nh + h + 1] - cum_c[i][:, nh + h:nh + h + 1] + ifc[i][:, h:h + 1]
              for h in heads] for i in range(nsub)]
    m_prev = [[[None] * nck for _ in heads] for _ in range(nsub)]
    m_next = [[[None] * nck for _ in heads] for _ in range(nsub)]
    for h in heads:
        m_run = m_ref[h:h + 1, 0:1]
        for i in range(nsub):
            for c, cr in enumerate(chunks):
                tot = tot_c[i][c * L:c * L + 1, nh + h:nh + h + 1]
                m_prev[i][h][c] = m_run
                m_run = jnp.maximum(tot + m_run, jnp.max(a_end[i][h][cr], axis=0, keepdims=True))
                m_next[i][h][c] = m_run
        m_ref[h:h + 1, :] = jnp.broadcast_to(m_run, (1, LANES))

    def per_row(vals):
        return jnp.concatenate([jnp.broadcast_to(t, (L, 1)) for t in vals], axis=0)

    qf = [[qkc_ref[rows, hcol(h)] for h in heads] for rows in subs]
    kf = [[qkc_ref[rows, MLSTM_DQK + h * dk:MLSTM_DQK + (h + 1) * dk] * (dk ** -0.5) for h in heads]
          for rows in subs]
    q = [[t.astype(BF16) for t in r] for r in qf]
    k = [[t.astype(BF16) for t in r] for r in kf]
    v = [[v_ref[rows, h * dv:(h + 1) * dv].astype(BF16) for h in heads] for rows in subs]
    qk = [[_dot(q[i][h], k[i][h], NT) for h in heads] for i in range(nsub)]

    kv_t = [[[None] * nck for _ in heads] for _ in range(nsub)]
    n_add = [[[None] * nck for _ in heads] for _ in range(nsub)]
    dec = [[[None] * nck for _ in heads] for _ in range(nsub)]
    for i in range(nsub):
        for h in heads:
            w = jnp.exp(a_end[i][h] - per_row(m_next[i][h]))
            wk = w * kf[i][h]
            for c, cr in enumerate(chunks):
                tot = tot_c[i][c * L:c * L + 1, nh + h:nh + h + 1]
                dec[i][h][c] = jnp.exp(tot + m_prev[i][h][c] - m_next[i][h][c])
                kv_t[i][h][c] = _dot(v[i][h][cr], wk[cr].astype(BF16), TN)
                n_add[i][h][c] = jnp.sum(wk[cr], axis=0, keepdims=True)

    w_inter = [[None] * nh for _ in range(nsub)]
    m_row = [[None] * nh for _ in range(nsub)]
    w_intra = [[None] * nh for _ in range(nsub)]
    for i in range(nsub):
        for h in heads:
            cum_i = cum_c[i][:, nh + h:nh + h + 1]
            log_d = jnp.where(causal, cum_i - cum_r[i][nh + h:nh + h + 1, :] + ifr[i][h:h + 1, :], neg)
            log_inter = cum_i + per_row(m_prev[i][h])
            m_i = jnp.maximum(log_inter, jnp.max(log_d, axis=1, keepdims=True))
            w_intra[i][h] = qk[i][h] * jnp.exp(log_d - m_i)
            w_inter[i][h] = jnp.exp(log_inter - m_i)
            m_row[i][h] = m_i
    num_intra = [[_dot(w_intra[i][h].astype(BF16), v[i][h]) for h in heads] for i in range(nsub)]

    c_prev = [[[None] * nck for _ in heads] for _ in range(nsub)]
    n_prev = [[[None] * nck for _ in heads] for _ in range(nsub)]
    for h in heads:
        ct = ct_ref[h]
        nv = n_ref[h:h + 1, :]
        for i in range(nsub):
            for c in range(nck):
                c_prev[i][h][c] = ct.astype(BF16)
                n_prev[i][h][c] = nv
                ct = dec[i][h][c] * ct + kv_t[i][h][c]
                nv = dec[i][h][c] * nv + n_add[i][h][c]
        ct_ref[h] = ct
        n_ref[h:h + 1, :] = nv

    for i, rows in enumerate(subs):
        outs = []
        for h in heads:
            inter = jnp.concatenate([_dot(q[i][h][cr], c_prev[i][h][c], NT)
                                     for c, cr in enumerate(chunks)], axis=0)
            qn = jnp.concatenate([jnp.sum(qf[i][h][cr] * n_prev[i][h][c], axis=1, keepdims=True)
                                  for c, cr in enumerate(chunks)], axis=0)
            num = w_inter[i][h] * inter + num_intra[i][h]
            den = w_inter[i][h] * qn + jnp.sum(w_intra[i][h], axis=1, keepdims=True)
            outs.append(num / jnp.maximum(jnp.abs(den), jnp.exp(-m_row[i][h])))
        og = og_ref[rows, :]
        o_ref[rows, :] = (jnp.concatenate(outs, axis=1) * _sigmoid(og)).astype(o_ref.dtype)


def mlstm_scan(z, ifr, conv_w, conv_b, b, s):
    t = z.shape[0]
    nc = s // SCAN_ROWS
    cpt = SCAN_ROWS // SUB
    dk = MLSTM_DQK // MLSTM_HEADS
    dv = MLSTM_DV // MLSTM_HEADS
    row = lambda w, col: pl.BlockSpec((SCAN_ROWS, w), lambda bi, c: (bi * nc + c, col))
    return pl.pallas_call(
        _mlstm_kernel,
        grid=(b, nc),
        in_specs=[row(2 * MLSTM_DQK, 0), row(MIX, 2), row(MIX, 3),
                  row(LANES, 4 * MIX // LANES),
                  pl.BlockSpec((cpt, 2 * MLSTM_HEADS, SUB), lambda bi, c: (bi * nc + c, 0, 0)),
                  _resident((MLSTM_CONV, 2 * MLSTM_DQK)), _resident((1, 2 * MLSTM_DQK))],
        out_specs=row(MIX, 0),
        out_shape=jax.ShapeDtypeStruct((t, MIX), BF16),
        scratch_shapes=[pltpu.VMEM((TAIL, 2 * MLSTM_DQK), F32),
                        pltpu.VMEM((MLSTM_HEADS, dv, dk), F32),
                        pltpu.VMEM((8, dk), F32),
                        pltpu.VMEM((8, LANES), F32),
                        pltpu.VMEM((SCAN_ROWS, 2 * MLSTM_DQK), F32)],
        compiler_params=_params("parallel", "arbitrary"),
        name="mlstm_scan",
    )(z, z, z, z, ifr, conv_w, conv_b.reshape(1, -1))


def _merge_kernel(x_ref, g_ref, wg_ref, bg_ref, ya_ref, yb_ref, yc_ref,
                  wa_ref, wb_ref, wc_ref, wo_ref, o_ref):
    x = x_ref[...]
    h = _rms(x, g_ref[...]).astype(BF16)
    d = D_MODEL
    merged = None
    for i, (y_ref, w_ref) in enumerate(((ya_ref, wa_ref), (yb_ref, wb_ref), (yc_ref, wc_ref))):
        gate = _sigmoid(_dot(h, wg_ref[:, i * d:(i + 1) * d]) + bg_ref[:, i * d:(i + 1) * d])
        term = gate * _dot(y_ref[...], w_ref[...])
        merged = term if merged is None else merged + term
    o_ref[...] = x + _dot(merged.astype(BF16), wo_ref[...])


def merge_out(x2d, g, w_gate, b_gate, ya, yb, yc, wa, wb, wc, wo, *, tm):
    t, d = x2d.shape
    row = lambda w: pl.BlockSpec((tm, w), lambda i: (i, 0))
    return pl.pallas_call(
        _merge_kernel,
        grid=(t // tm,),
        in_specs=[row(d), _resident((1, d)), _resident((d, N_BRANCHES * d)),
                  _resident((1, N_BRANCHES * d)), row(MIX), row(MIX), row(MIX),
                  _resident((MIX, d)), _resident((MIX, d)), _resident((MIX, d)),
                  _resident((d, d))],
        out_specs=row(d),
        out_shape=jax.ShapeDtypeStruct((t, d), F32),
        compiler_params=_params("parallel"),
        name="merge_out",
    )(x2d, g.reshape(1, d), w_gate, b_gate.reshape(1, -1), ya, yb, yc, wa, wb, wc, wo)


def _dense_ffn_kernel(x_ref, g_ref, wg_ref, wu_ref, wd_ref, o_ref):
    x = x_ref[...]
    h = _rms(x, g_ref[...]).astype(BF16)
    gt = _dot(h, wg_ref[...])
    up = _dot(h, wu_ref[...])
    act = (gt * _sigmoid(gt) * up).astype(BF16)
    o_ref[...] = x + _dot(act, wd_ref[...])


def dense_ffn(x2d, g, wg, wu, wd, *, tm):
    t, d = x2d.shape
    ff = wg.shape[1]
    return pl.pallas_call(
        _dense_ffn_kernel,
        grid=(t // tm,),
        in_specs=[pl.BlockSpec((tm, d), lambda i: (i, 0)), _resident((1, d)),
                  _resident((d, ff)), _resident((d, ff)), _resident((ff, d))],
        out_specs=pl.BlockSpec((tm, d), lambda i: (i, 0)),
        out_shape=jax.ShapeDtypeStruct((t, d), F32),
        compiler_params=_params("parallel"),
        name="dense_ffn",
    )(x2d, g.reshape(1, d), wg, wu, wd)


ROUTER_TM = 512
MOE_ROWS = 512
META_E, META_G, META_RANK = 0, 2, 4


def _router_kernel(x_ref, g_ref, rw_ref, rb_ref, h_ref, meta_ref, cnt_ref, carry_ref):
    @pl.when(pl.program_id(0) == 0)
    def _():
        carry_ref[...] = jnp.zeros_like(carry_ref)

    h = _rms(x_ref[...], g_ref[...])
    h_ref[...] = h
    tm = h.shape[0]
    lane = lax.broadcasted_iota(I32, (tm, LANES), 1)
    neg = -jnp.inf
    logits = jnp.where(lane < N_EXPERTS, _dot_f32(h, rw_ref[...]) + rb_ref[...], neg)
    m1 = jnp.max(logits, axis=1, keepdims=True)
    e1 = jnp.min(jnp.where(logits == m1, lane, LANES), axis=1, keepdims=True)
    rest = jnp.where(lane == e1, neg, logits)
    m2 = jnp.max(rest, axis=1, keepdims=True)
    e2 = jnp.min(jnp.where(rest == m2, lane, LANES), axis=1, keepdims=True)
    ex = jnp.exp(m2 - m1)
    g1 = 1.0 / (1.0 + ex)
    g2 = ex / (1.0 + ex)
    oh1 = jnp.where(lane == e1, 1.0, 0.0)
    oh2 = jnp.where(lane == e2, 1.0, 0.0)
    cnt = oh1 + oh2
    ri = lax.broadcasted_iota(I32, (tm, tm), 0)
    ci = lax.broadcasted_iota(I32, (tm, tm), 1)
    strict = jnp.where(ci < ri, 1.0, 0.0).astype(BF16)
    excl = _dot(strict, cnt.astype(BF16)) + carry_ref[...]
    r1 = jnp.sum(excl * oh1, axis=1, keepdims=True)
    r2 = jnp.sum(excl * oh2, axis=1, keepdims=True)
    carry_ref[...] += jnp.sum(cnt, axis=0, keepdims=True)
    cnt_ref[...] = carry_ref[...]
    meta = jnp.zeros((tm, LANES), F32)
    for pos, val in ((META_E, e1.astype(F32)), (META_E + 1, e2.astype(F32)),
                     (META_G, g1), (META_G + 1, g2), (META_RANK, r1), (META_RANK + 1, r2)):
        meta = jnp.where(lane == pos, val, meta)
    meta_ref[...] = meta


def moe_router(x2d, g, rw_pad, rb_pad):
    t, d = x2d.shape
    tm = ROUTER_TM
    return pl.pallas_call(
        _router_kernel,
        grid=(t // tm,),
        in_specs=[pl.BlockSpec((tm, d), lambda i: (i, 0)), _resident((1, d)),
                  _resident((d, LANES)), _resident((1, LANES))],
        out_specs=[pl.BlockSpec((tm, d), lambda i: (i, 0)),
                   pl.BlockSpec((tm, LANES), lambda i: (i, 0)),
                   pl.BlockSpec((1, LANES), lambda i: (0, 0))],
        out_shape=[jax.ShapeDtypeStruct((t, d), F32), jax.ShapeDtypeStruct((t, LANES), F32),
                   jax.ShapeDtypeStruct((1, LANES), F32)],
        scratch_shapes=[pltpu.VMEM((1, LANES), F32)],
        compiler_params=_params("arbitrary"),
        name="moe_router",
    )(x2d, g.reshape(1, d), rw_pad, rb_pad)


DISPATCH_TM = 1024
ISSUE_UNROLL = 8


def _row_copy(src_ref, dst_ref, src_row, dst_row, sem):
    return pltpu.make_async_copy(src_ref.at[pl.ds(src_row, 1)], dst_ref.at[pl.ds(dst_row, 1)], sem)


def _dispatch_kernel(d1_ref, d2_ref, h_ref, init_hbm, xs_hbm, sem):
    del init_hbm
    n = h_ref.shape[0]
    per_trip = ISSUE_UNROLL // TOP_K

    def start(i, _):
        for u in range(per_trip):
            r = i * per_trip + u
            _row_copy(h_ref, xs_hbm, r, d1_ref[0, 0, r], sem).start(priority=0)
            _row_copy(h_ref, xs_hbm, r, d2_ref[0, 0, r], sem).start(priority=1)
        return 0

    def wait(i, _):
        for u in range(ISSUE_UNROLL):
            _row_copy(h_ref, xs_hbm, 0, 0, sem).wait()
        return 0

    lax.fori_loop(0, n // per_trip, start, 0)
    lax.fori_loop(0, n // per_trip, wait, 0)


def dispatch_rows(h, dest1, dest2, n_rows):
    t, d = h.shape
    tm = DISPATCH_TM
    nblk = t // tm
    idx = pl.BlockSpec((1, 1, tm), lambda i: (i, 0, 0), memory_space=pltpu.SMEM)
    return pl.pallas_call(
        _dispatch_kernel,
        grid=(nblk,),
        in_specs=[idx, idx, pl.BlockSpec((tm, d), lambda i: (i, 0)),
                  pl.BlockSpec(memory_space=pl.ANY)],
        out_specs=pl.BlockSpec(memory_space=pl.ANY),
        out_shape=jax.ShapeDtypeStruct((n_rows, d), h.dtype),
        input_output_aliases={3: 0},
        scratch_shapes=[pltpu.SemaphoreType.DMA(())],
        compiler_params=_params("arbitrary"),
        name="dispatch_rows",
    )(dest1.reshape(nblk, 1, tm), dest2.reshape(nblk, 1, tm), h, jnp.zeros((n_rows, d), h.dtype))


MXU_DEPTH = 256


def _ff_splits(ff):
    assert ff % MXU_DEPTH == 0
    mid = (ff // MXU_DEPTH + 1) // 2 * MXU_DEPTH
    return ((0, mid), (mid, ff)) if mid < ff else ((0, ff),)


def _grouped_ffn_kernel(be_ref, bv_ref, x_ref, wg_ref, wu_ref, wd_ref, o_ref):
    i = pl.program_id(0)

    @pl.when(bv_ref[i] > 0)
    def _():
        h = x_ref[...].astype(BF16)
        acc = None
        for lo, hi in _ff_splits(wg_ref.shape[1]):
            gt = _dot(h, wg_ref[:, lo:hi])
            up = _dot(h, wu_ref[:, lo:hi])
            act = (gt * _sigmoid(gt) * up).astype(BF16)
            part = _dot(act, wd_ref[lo:hi, :])
            acc = part if acc is None else acc + part
        o_ref[...] = acc

    @pl.when(bv_ref[i] == 0)
    def _():
        o_ref[...] = jnp.zeros_like(o_ref)


def grouped_ffn(xs, blk_e, blk_valid, wg, wu, wd):
    n, d = xs.shape
    ff = wg.shape[2]
    nblk = n // MOE_ROWS
    grid_spec = pltpu.PrefetchScalarGridSpec(
        num_scalar_prefetch=2,
        grid=(nblk,),
        in_specs=[pl.BlockSpec((MOE_ROWS, d), lambda i, be, bv: (i, 0)),
                  pl.BlockSpec((None, d, ff), lambda i, be, bv: (be[i], 0, 0)),
                  pl.BlockSpec((None, d, ff), lambda i, be, bv: (be[i], 0, 0)),
                  pl.BlockSpec((None, ff, d), lambda i, be, bv: (be[i], 0, 0))],
        out_specs=pl.BlockSpec((MOE_ROWS, d), lambda i, be, bv: (i, 0)),
    )
    return pl.pallas_call(
        _grouped_ffn_kernel,
        grid_spec=grid_spec,
        out_shape=jax.ShapeDtypeStruct((n, d), F32),
        compiler_params=_params("arbitrary"),
        name="grouped_ffn",
    )(blk_e, blk_valid, xs, wg, wu, wd)


COMBINE_TM = 1024


def _combine_kernel(d1_ref, d2_ref, x_ref, meta_ref, fg_ref, ys_hbm, o_ref, b1_ref, b2_ref, sem):
    n = x_ref.shape[0]

    def start(i, _):
        for u in range(ISSUE_UNROLL // 2):
            r = i * (ISSUE_UNROLL // 2) + u
            _row_copy(ys_hbm, b1_ref, d1_ref[0, 0, r], r, sem.at[0]).start(priority=0)
            _row_copy(ys_hbm, b2_ref, d2_ref[0, 0, r], r, sem.at[1]).start(priority=1)
        return 0

    def wait(i, _):
        for u in range(ISSUE_UNROLL // 2):
            r = i * (ISSUE_UNROLL // 2) + u
            _row_copy(ys_hbm, b1_ref, 0, r, sem.at[0]).wait()
            _row_copy(ys_hbm, b2_ref, 0, r, sem.at[1]).wait()
        return 0

    lax.fori_loop(0, n // (ISSUE_UNROLL // 2), start, 0)
    lax.fori_loop(0, n // (ISSUE_UNROLL // 2), wait, 0)
    meta = meta_ref[...]
    g1 = meta[:, META_G:META_G + 1]
    g2 = meta[:, META_G + 1:META_G + 2]
    y = x_ref[...] + (g1 * b1_ref[...] + g2 * b2_ref[...])
    o_ref[...] = _rms(y, fg_ref[...])


def moe_combine_norm(x2d, meta, dest1, dest2, ys, final_g):
    t, d = x2d.shape
    tm = COMBINE_TM
    nblk = t // tm
    idx = pl.BlockSpec((1, 1, tm), lambda i: (i, 0, 0), memory_space=pltpu.SMEM)
    return pl.pallas_call(
        _combine_kernel,
        grid=(nblk,),
        in_specs=[idx, idx, pl.BlockSpec((tm, d), lambda i: (i, 0)),
                  pl.BlockSpec((tm, LANES), lambda i: (i, 0)), _resident((1, d)),
                  pl.BlockSpec(memory_space=pl.ANY)],
        out_specs=pl.BlockSpec((tm, d), lambda i: (i, 0)),
        out_shape=jax.ShapeDtypeStruct((t, d), F32),
        scratch_shapes=[pltpu.VMEM((tm, d), F32), pltpu.VMEM((tm, d), F32),
                        pltpu.SemaphoreType.DMA((2,))],
        compiler_params=_params("arbitrary"),
        name="moe_combine_norm",
    )(dest1.reshape(nblk, 1, tm), dest2.reshape(nblk, 1, tm), x2d, meta,
      final_g.reshape(1, d), ys)


def moe_layer_final(x2d, norm_g, router_w, router_b, wg, wu, wd, final_g):
    t, d = x2d.shape
    rw_pad = jnp.pad(router_w, ((0, 0), (0, LANES - N_EXPERTS)))
    rb_pad = jnp.pad(router_b, (0, LANES - N_EXPERTS)).reshape(1, LANES)
    h, meta, counts = moe_router(x2d, norm_g, rw_pad, rb_pad)

    counts = counts[0, :N_EXPERTS].astype(I32)
    padded = (counts + MOE_ROWS - 1) // MOE_ROWS * MOE_ROWS
    pad_ends = jnp.cumsum(padded)
    pad_starts = pad_ends - padded
    e = meta[:, META_E:META_E + TOP_K].astype(I32)
    rank = meta[:, META_RANK:META_RANK + TOP_K].astype(I32)
    dest = pad_starts[e] + rank
    n_blk = (t * TOP_K) // MOE_ROWS + N_EXPERTS
    n_rows = n_blk * MOE_ROWS
    blk_start = jnp.arange(n_blk, dtype=I32) * MOE_ROWS
    blk_valid = (blk_start < pad_ends[-1]).astype(I32)
    blk_e = jnp.minimum(jnp.sum((blk_start[:, None] >= pad_ends[None, :]).astype(I32), axis=1),
                        N_EXPERTS - 1)

    xs = dispatch_rows(h, dest[:, 0], dest[:, 1], n_rows)
    ys = grouped_ffn(xs, blk_e, blk_valid, wg, wu, wd)
    return moe_combine_norm(x2d, meta, dest[:, 0], dest[:, 1], ys, final_g)


def _col_slices():
    sizes = (('moba', 3 * MIX), ('gla_qkv', 2 * GLA_DK + GLA_DV), ('gla_a', GLA_GATE_RANK),
             ('gla_r', GLA_DV), ('mlstm_qkv', 2 * MLSTM_DQK + MLSTM_DV),
             ('mlstm_if', 2 * MLSTM_HEADS), ('mlstm_o', MLSTM_DV),
             ('merge_gate', N_BRANCHES * D_MODEL))
    out, off = {}, 0
    for name, size in sizes:
        out[name] = slice(off, off + size)
        off += size
    return out


def _pad_cols(a, width):
    return jnp.pad(a, ((0, 0), (0, width - a.shape[1])))


def hybrid_mixer_layer(x2d, b, s, norm_g, w_in, b_in, conv_w, conv_b, gate_w, gate_b, gla_norm_g,
                       w_up_moba, w_up_gla, w_up_mlstm, w_o, tables):
    cs = _col_slices()
    bias = b_in.reshape(1, -1)
    wcat = lambda names: jnp.concatenate([w_in[:, cs[n]] for n in names], axis=1)
    bcat = lambda names: jnp.concatenate([bias[:, cs[n]] for n in names], axis=1)

    w_gla = jnp.concatenate([wcat(['gla_qkv', 'gla_r']), _pad_cols(w_in[:, cs['gla_a']], LANES)], axis=1)
    b_gla = jnp.concatenate([bcat(['gla_qkv', 'gla_r']), _pad_cols(bias[:, cs['gla_a']], LANES)], axis=1)
    w_ml = jnp.concatenate([wcat(['mlstm_qkv', 'mlstm_o']), _pad_cols(w_in[:, cs['mlstm_if']], LANES)], axis=1)
    b_ml = jnp.concatenate([bcat(['mlstm_qkv', 'mlstm_o']), _pad_cols(bias[:, cs['mlstm_if']], LANES)], axis=1)
    q, k, vt, km, z_gla, z_ml = mixer_proj(
        x2d, norm_g, w_in[:, cs['moba']].astype(BF16), bias[:, cs['moba']],
        w_gla.astype(BF16), b_gla, w_ml.astype(BF16), b_ml, tables, s, tm=512)

    nb = s // MOBA_BLOCK
    y_moba = moba_attention(q.reshape(b, s, MIX), k.reshape(b, s, MIX),
                            vt.reshape(b, nb, MIX, MOBA_BLOCK),
                            km.reshape(b, nb, MIX)).reshape(b * s, MIX)

    gate_w_pad = jnp.pad(gate_w, ((0, LANES - GLA_GATE_RANK), (0, 0)))
    y_gla = gla_scan(z_gla, gate_w_pad, gate_b, gla_norm_g, b, s)

    n_if = 2 * MLSTM_HEADS
    ifr = z_ml[:, 4 * MIX:4 * MIX + n_if].reshape(b * s // SUB, SUB, n_if).transpose(0, 2, 1)
    y_ml = mlstm_scan(z_ml, ifr, conv_w, conv_b, b, s)

    return merge_out(x2d, norm_g, w_in[:, cs['merge_gate']].astype(BF16), bias[:, cs['merge_gate']],
                     y_moba, y_gla, y_ml, w_up_moba.astype(BF16), w_up_gla.astype(BF16),
                     w_up_mlstm.astype(BF16), w_o.astype(BF16), tm=512)


def kernel(x, attn_norm_g, w_in, b_in, mlstm_conv_w, mlstm_conv_b, gla_gate_w, gla_gate_b, gla_norm_g, w_up_moba, w_up_gla, w_up_mlstm, w_o, ffn_norm_g, ffn_w_gate, ffn_w_up, ffn_w_down, router_w, router_b, moe_w_gate, moe_w_up, moe_w_down, final_norm_g):
    b, s, d = x.shape
    depth = w_in.shape[0]
    assert d == D_MODEL and depth == 2, "two layers: dense SwiGLU then MoE"
    assert s % SCAN_ROWS == 0 and s % MOBA_BLOCK == 0
    x2d = x.reshape(b * s, d)
    tables = _rope_tables(s)
    for l in range(depth):
        x2d = hybrid_mixer_layer(x2d, b, s, attn_norm_g[l], w_in[l], b_in[l], mlstm_conv_w[l],
                                 mlstm_conv_b[l], gla_gate_w[l], gla_gate_b[l], gla_norm_g[l],
                                 w_up_moba[l], w_up_gla[l], w_up_mlstm[l], w_o[l], tables)
        j = l // 2
        if l % 2 == 0:
            x2d = dense_ffn(x2d, ffn_norm_g[l], ffn_w_gate[j].astype(BF16), ffn_w_up[j].astype(BF16),
                            ffn_w_down[j].astype(BF16), tm=512)
        else:
            x2d = moe_layer_final(x2d, ffn_norm_g[l], router_w[j], router_b[j],
                                  moe_w_gate[j].astype(BF16), moe_w_up[j].astype(BF16),
                                  moe_w_down[j].astype(BF16), final_norm_g)
    return x2d.reshape(b, s, d)
```

```python
import jax
import jax.numpy as jnp
from jax import lax
from jax.experimental import pallas as pl
from jax.experimental.pallas import tpu as pltpu

F32 = jnp.float32
BF16 = jnp.bfloat16
I32 = jnp.int32

D_MODEL = 1024
MIX = D_MODEL // 2
MOBA_HEAD_DIM = 64
MOBA_BLOCK = 256
MOBA_TOPK = 3
ROPE_THETA = 500000.0
ROPE_DIMS = MOBA_HEAD_DIM // 4
GLA_HEADS = 4
GLA_DK = MIX // 2
GLA_DV = MIX
GLA_GATE_RANK = 16
GLA_TAU = 16.0
MLSTM_HEADS = 4
MLSTM_DQK = MIX
MLSTM_DV = MIX
MLSTM_CONV = 4
CHUNK = 64
N_BRANCHES = 3
N_EXPERTS = 8
TOP_K = 2
EPS = 1e-6
LOG2E = 1.4426950408889634

LANES = 128
VMEM_LIMIT = 56 * 1024 * 1024

NN = (((1,), (0,)), ((), ()))
NT = (((1,), (1,)), ((), ()))
TN = (((0,), (0,)), ((), ()))


def _dot(a, b, dims=NN):
    return lax.dot_general(a, b, dims, preferred_element_type=F32)


def _dot_f32(a, b, dims=NN):
    return lax.dot_general(a, b, dims, preferred_element_type=F32,
                           precision=lax.Precision.HIGHEST)


def _rms(x, g):
    return x * lax.rsqrt(jnp.mean(x * x, axis=-1, keepdims=True) + EPS) * g


def _sigmoid(x):
    return 1.0 / (1.0 + jnp.exp(-x))


def _log_sigmoid(x):
    return jnp.minimum(x, 0.0) - jnp.log(1.0 + jnp.exp(-jnp.abs(x)))


def _params(*sem):
    return pltpu.CompilerParams(dimension_semantics=sem, vmem_limit_bytes=VMEM_LIMIT)


def _resident(shape):
    n = len(shape)
    return pl.BlockSpec(shape, lambda *_: (0,) * n, pipeline_mode=pl.Buffered(1))


def _rope_tables(s):
    half = ROPE_DIMS // 2
    inv_freq = jnp.power(ROPE_THETA, -jnp.arange(half, dtype=F32) * 2.0 / ROPE_DIMS)
    ang = jnp.arange(s).astype(F32)[:, None] * inv_freq[None, :]
    cos, sin = jnp.cos(ang), jnp.sin(ang)
    ones = jnp.ones((s, MOBA_HEAD_DIM - ROPE_DIMS), F32)
    zeros = jnp.zeros((s, MOBA_HEAD_DIM - ROPE_DIMS), F32)
    zh = jnp.zeros((s, half), F32)
    c = jnp.concatenate([cos, cos, ones], axis=1)
    sa = jnp.concatenate([zh, sin, zeros], axis=1)
    sb = jnp.concatenate([-sin, zh, zeros], axis=1)
    rep = LANES // MOBA_HEAD_DIM
    return tuple(jnp.tile(a, (1, rep)) for a in (c, sa, sb))


def _mixer_proj_kernel(x_ref, g_ref, wa_ref, ba_ref, wb_ref, bb_ref, wc_ref, bc_ref,
                       c_ref, sa_ref, sb_ref,
                       q_ref, k_ref, vt_ref, km_ref, zb_ref, zc_ref):
    h = _rms(x_ref[...], g_ref[...]).astype(BF16)
    zb_ref[...] = _dot(h, wb_ref[...]) + bb_ref[...]
    zc_ref[...] = _dot(h, wc_ref[...]) + bc_ref[...]
    z = _dot(h, wa_ref[...]) + ba_ref[...]
    tm = z.shape[0]
    rep = MIX // LANES
    c = jnp.concatenate([c_ref[...]] * rep, axis=1)
    sa = jnp.concatenate([sa_ref[...]] * rep, axis=1)
    sb = jnp.concatenate([sb_ref[...]] * rep, axis=1)
    half = ROPE_DIMS // 2

    def rot(t):
        return t * c + pltpu.roll(t, half, 1) * sa + pltpu.roll(t, MIX - half, 1) * sb

    q = rot(z[:, :MIX]) * (MOBA_HEAD_DIM ** -0.5 * LOG2E)
    k = rot(z[:, MIX:2 * MIX])
    q_ref[...] = q.astype(BF16)
    k_ref[...] = k.astype(BF16)
    nb = tm // MOBA_BLOCK
    for i in range(nb):
        rows = slice(i * MOBA_BLOCK, (i + 1) * MOBA_BLOCK)
        vt_ref[i] = z[rows, 2 * MIX:].T.astype(BF16)
        km_ref[i] = jnp.mean(k[rows, :], axis=0, keepdims=True)


def mixer_proj(x2d, g, wa, ba, wb, bb, wc, bc, tables, s, *, tm):
    t, d = x2d.shape
    c, sa, sb = tables
    spt = s // tm
    tab = pl.BlockSpec((tm, LANES), lambda i: (i % spt, 0))
    row = lambda w: pl.BlockSpec((tm, w), lambda i: (i, 0))
    nb = tm // MOBA_BLOCK
    na, nbw, ncw = wa.shape[1], wb.shape[1], wc.shape[1]
    return pl.pallas_call(
        _mixer_proj_kernel,
        grid=(t // tm,),
        in_specs=[row(d), _resident((1, d)),
                  _resident((d, na)), _resident((1, na)),
                  _resident((d, nbw)), _resident((1, nbw)),
                  _resident((d, ncw)), _resident((1, ncw)), tab, tab, tab],
        out_specs=[row(MIX), row(MIX), pl.BlockSpec((nb, MIX, MOBA_BLOCK), lambda i: (i, 0, 0)),
                   pl.BlockSpec((nb, 1, MIX), lambda i: (i, 0, 0)), row(nbw), row(ncw)],
        out_shape=[jax.ShapeDtypeStruct((t, MIX), BF16)] * 2
        + [jax.ShapeDtypeStruct((t // MOBA_BLOCK, MIX, MOBA_BLOCK), BF16),
           jax.ShapeDtypeStruct((t // MOBA_BLOCK, 1, MIX), F32),
           jax.ShapeDtypeStruct((t, nbw), F32), jax.ShapeDtypeStruct((t, ncw), F32)],
        compiler_params=_params("parallel"),
        name="mixer_proj",
    )(x2d, g.reshape(1, d), wa, ba, wb, bb, wc, bc, c, sa, sb)


MOBA_TQ = MOBA_BLOCK
MOBA_LANES = 256
HEADS_PER_STEP = MOBA_LANES // MOBA_HEAD_DIM
SUM_ROWS = 16


def _moba_attn_kernel(q_ref, k_ref, vt_ref, km_ref, o_ref, bias_ref, qh_ref, s_ref, p_ref):
    cur = pl.program_id(2)
    nb = km_ref.shape[1]
    hd = MOBA_HEAD_DIM
    neg = -jnp.inf
    own = pl.multiple_of(cur * MOBA_BLOCK, MOBA_BLOCK)
    nh = HEADS_PER_STEP

    ones_rows = jnp.ones((SUM_ROWS, MOBA_BLOCK), BF16)

    def v_rows(blk, hh):
        return jnp.concatenate([vt_ref[0, blk, hh * hd:(hh + 1) * hd, :], ones_rows], axis=0)

    def scores_into(slot, blk):
        off = pl.multiple_of(blk * MOBA_BLOCK, MOBA_BLOCK)
        kb = k_ref[0, pl.ds(off, MOBA_BLOCK), :]
        for hh in range(nh):
            s_ref[slot, hh] = _dot(kb, qh_ref[hh], NT)

    q_all = q_ref[0]
    lane = lax.broadcasted_iota(I32, q_all.shape, 1)
    km = km_ref[0]
    km_hi = km.astype(BF16)
    km_lo = (km - km_hi.astype(F32)).astype(BF16)
    k_own = k_ref[0, pl.ds(own, MOBA_BLOCK), :]
    k_first = k_ref[0, 0:MOBA_BLOCK, :]
    init = []
    for hh in range(nh):
        q = jnp.where((lane >= hh * hd) & (lane < (hh + 1) * hd), q_all, jnp.zeros_like(q_all))
        qh_ref[hh] = q
        lhs = jnp.concatenate([km_hi, km_lo, k_own, k_first], axis=0)
        prod = _dot(lhs, q, NT)
        gate = prod[:nb] + prod[nb:2 * nb]
        blk = lax.broadcasted_iota(I32, gate.shape, 0)
        gate = jnp.where(blk < cur, gate, neg)
        sel = jnp.zeros(gate.shape, F32)
        for _ in range(MOBA_TOPK):
            m = jnp.max(gate, axis=0, keepdims=True)
            first = jnp.min(jnp.where(gate == m, blk, nb), axis=0, keepdims=True)
            first = jnp.where(m > neg, first, nb)
            pick = blk == first
            sel = jnp.where(pick, 1.0, sel)
            gate = jnp.where(pick, neg, gate)
        bias_ref[hh] = jnp.where(sel > 0.0, 0.0, neg)

        s = prod[2 * nb:2 * nb + MOBA_BLOCK]
        s_ref[0, hh] = prod[2 * nb + MOBA_BLOCK:]
        k_pos = lax.broadcasted_iota(I32, s.shape, 0)
        q_pos = lax.broadcasted_iota(I32, s.shape, 1)
        s = jnp.where(k_pos <= q_pos, s, neg)
        m0 = jnp.max(s, axis=0, keepdims=True)
        p = jnp.exp2(s - m0)
        p_ref[hh] = p.astype(BF16)
        init.append((m0, jnp.zeros((hd + SUM_ROWS, MOBA_TQ), F32), jnp.ones((1, MOBA_TQ), F32)))

    def step(j, slot, carry):
        pend = jnp.where(j == 0, cur, j - 1)
        pv = [_dot(v_rows(pend, hh), p_ref[hh]) for hh in range(nh)]
        out = []
        for hh in range(nh):
            m_i, acc, a_pend = carry[hh]
            bias = bias_ref[hh, pl.ds(j, 1), :]
            m_new = jnp.maximum(m_i, jnp.max(s_ref[slot, hh], axis=0, keepdims=True) + bias)
            alpha = jnp.exp2(m_i - m_new)
            acc_new = a_pend * acc + pv[hh]
            p_ref[hh] = jnp.exp2(s_ref[slot, hh] + (bias - m_new)).astype(BF16)
            out.append((m_new, acc_new, alpha))
        scores_into(1 - slot, jnp.minimum(j + 1, nb - 1))
        return tuple(out)

    def body(i, carry):
        return step(2 * i + 1, 1, step(2 * i, 0, carry))

    trips = (cur + 1) // 2
    final = lax.fori_loop(0, trips, body, tuple(init))
    pend = jnp.where(trips == 0, cur, 2 * trips - 1)
    outs = []
    for hh in range(nh):
        _, acc, a_pend = final[hh]
        acc = a_pend * acc + _dot(v_rows(pend, hh), p_ref[hh])
        outs.append((acc[:hd] / acc[hd:hd + 1]).T)
    o_ref[0] = jnp.concatenate(outs, axis=1).astype(o_ref.dtype)


def moba_attention(q, k, vt, km):
    b, s, _ = q.shape
    nb = km.shape[1]
    qspec = pl.BlockSpec((1, MOBA_TQ, MOBA_LANES), lambda bi, hp, qi: (bi, qi, hp))
    return pl.pallas_call(
        _moba_attn_kernel,
        grid=(b, MIX // MOBA_LANES, s // MOBA_TQ),
        in_specs=[qspec,
                  pl.BlockSpec((1, s, MOBA_LANES), lambda bi, hp, qi: (bi, 0, hp)),
                  pl.BlockSpec((1, nb, MOBA_LANES, MOBA_BLOCK), lambda bi, hp, qi: (bi, 0, hp, 0)),
                  pl.BlockSpec((1, nb, MOBA_LANES), lambda bi, hp, qi: (bi, 0, hp))],
        out_specs=qspec,
        out_shape=jax.ShapeDtypeStruct((b, s, MIX), BF16),
        scratch_shapes=[pltpu.VMEM((HEADS_PER_STEP, nb, MOBA_TQ), F32),
                        pltpu.VMEM((HEADS_PER_STEP, MOBA_TQ, MOBA_LANES), BF16),
                        pltpu.VMEM((2, HEADS_PER_STEP, MOBA_BLOCK, MOBA_TQ), F32),
                        pltpu.VMEM((HEADS_PER_STEP, MOBA_BLOCK, MOBA_TQ), BF16)],
        compiler_params=_params("parallel", "parallel", "arbitrary"),
        name="moba_attention",
    )(q, k, vt, km)


SCAN_ROWS = 512


SUB = 128


def _split3(x):
    hi = x.astype(BF16)
    r1 = x - hi.astype(F32)
    mid = r1.astype(BF16)
    lo = (r1 - mid.astype(F32)).astype(BF16)
    return hi, mid, lo


def _dot_exact_lhs(lhs_bf16, x):
    return sum(_dot(lhs_bf16, t) for t in _split3(x))


def _chunk_masks(n):
    ri = lax.broadcasted_iota(I32, (n, n), 0)
    ci = lax.broadcasted_iota(I32, (n, n), 1)
    same = (ri // CHUNK) == (ci // CHUNK)
    return same, same & (ci <= ri)


def _gla_kernel(qk_ref, v_ref, r_ref, a_ref, gw_ref, gb_ref, ng_ref, o_ref, st_ref):
    @pl.when(pl.program_id(1) == 0)
    def _():
        st_ref[...] = jnp.zeros_like(st_ref)

    dk = GLA_DK // GLA_HEADS
    dv = GLA_DV // GLA_HEADS
    L = CHUNK
    same, causal = _chunk_masks(SUB)
    tri = jnp.where(causal, 1.0, 0.0).astype(BF16)
    blk = jnp.where(same, 1.0, 0.0).astype(BF16)
    lane = lax.broadcasted_iota(I32, (SUB, LANES), 1)
    gw = gw_ref[...]
    gb = gb_ref[...]
    ng = ng_ref[...]
    nsub = SCAN_ROWS // SUB
    subs = [slice(i * SUB, (i + 1) * SUB) for i in range(nsub)]
    heads = range(GLA_HEADS)
    chunks = [slice(c * L, (c + 1) * L) for c in range(SUB // L)]
    pair = lambda h: slice((h // 2) * LANES, (h // 2 + 1) * LANES)

    log_a = [_log_sigmoid(_dot_f32(a_ref[rows, :], gw) + gb) / GLA_TAU for rows in subs]
    cum = [_dot_exact_lhs(tri, x) for x in log_a]
    tot = [_dot_exact_lhs(blk, x) for x in log_a]
    q_dec, k_inv, k_end, dec, v = [], [], [], [], []
    for i, rows in enumerate(subs):
        qk = qk_ref[rows, :]
        k = qk[:, GLA_DK:]
        q_dec.append((qk[:, :GLA_DK] * (dk ** -0.5) * jnp.exp(cum[i])).astype(BF16))
        k_inv.append((k * jnp.exp(-cum[i])).astype(BF16))
        k_end.append((k * jnp.exp(tot[i] - cum[i])).astype(BF16))
        dec.append(jnp.exp(tot[i]))
        v.append(v_ref[rows, :].astype(BF16))

    def q_masked(i, h):
        lo_lane = (h % 2) * dk
        return jnp.where((lane >= lo_lane) & (lane < lo_lane + dk), q_dec[i][:, pair(h)],
                         jnp.zeros((SUB, LANES), BF16))

    qm = [[q_masked(i, h) for h in heads] for i in range(nsub)]
    vh = [[v[i][:, h * dv:(h + 1) * dv] for h in heads] for i in range(nsub)]
    att = [[_dot(qm[i][h], k_inv[i][:, pair(h)], NT) for h in heads] for i in range(nsub)]
    kv_t = [[[_dot(vh[i][h][cr], k_end[i][cr, pair(h)], TN) for cr in chunks] for h in heads]
            for i in range(nsub)]
    intra = [[_dot(jnp.where(causal, att[i][h], 0.0).astype(BF16), vh[i][h]) for h in heads]
             for i in range(nsub)]

    prev = [[[None] * len(chunks) for _ in heads] for _ in range(nsub)]
    for h in heads:
        state = st_ref[h]
        for i in range(nsub):
            for c in range(len(chunks)):
                prev[i][h][c] = state.astype(BF16)
                state = state * dec[i][c * L:c * L + 1, pair(h)] + kv_t[i][h][c]
        st_ref[h] = state

    for i, rows in enumerate(subs):
        outs = []
        for h in heads:
            inter = [_dot(qm[i][h][cr], prev[i][h][c], NT) for c, cr in enumerate(chunks)]
            o = intra[i][h] + jnp.concatenate(inter, axis=0)
            outs.append(o * lax.rsqrt(jnp.mean(o * o, axis=-1, keepdims=True) + EPS))
        r = r_ref[rows, :]
        o_ref[rows, :] = (jnp.concatenate(outs, axis=1) * ng * (r * _sigmoid(r))).astype(o_ref.dtype)


def gla_scan(z, gate_w_pad, gate_b, norm_g, b, s):
    t = z.shape[0]
    nc = s // SCAN_ROWS
    wide = lambda col: pl.BlockSpec((SCAN_ROWS, MIX), lambda bi, c: (bi * nc + c, col))
    dk = GLA_DK // GLA_HEADS
    dv = GLA_DV // GLA_HEADS
    return pl.pallas_call(
        _gla_kernel,
        grid=(b, nc),
        in_specs=[wide(0), wide(1), wide(2),
                  pl.BlockSpec((SCAN_ROWS, LANES), lambda bi, c: (bi * nc + c, 3 * MIX // LANES)),
                  _resident((LANES, GLA_DK)), _resident((1, GLA_DK)), _resident((1, GLA_DV))],
        out_specs=pl.BlockSpec((SCAN_ROWS, MIX), lambda bi, c: (bi * nc + c, 0)),
        out_shape=jax.ShapeDtypeStruct((t, MIX), BF16),
        scratch_shapes=[pltpu.VMEM((GLA_HEADS, dv, LANES), F32)],
        compiler_params=_params("parallel", "arbitrary"),
        name="gla_scan",
    )(z, z, z, z, gate_w_pad, gate_b.reshape(1, GLA_DK), norm_g.reshape(1, GLA_DV))


TAIL = 8


def _mlstm_kernel(qk_ref, v_ref, og_ref, ifc_ref, ifr_ref, cw_ref, cb_ref, o_ref,
                  tail_ref, ct_ref, n_ref, m_ref, qkc_ref):
    @pl.when(pl.program_id(1) == 0)
    def _():
        tail_ref[...] = jnp.zeros_like(tail_ref)
        ct_ref[...] = jnp.zeros_like(ct_ref)
        n_ref[...] = jnp.zeros_like(n_ref)
        m_ref[...] = jnp.zeros_like(m_ref)

    nh = MLSTM_HEADS
    dk = MLSTM_DQK // nh
    dv = MLSTM_DV // nh
    L = CHUNK
    rows_n = qk_ref.shape[0]
    neg = -jnp.inf

    x = qk_ref[...]
    xx = jnp.concatenate([tail_ref[...], x], axis=0)
    cw = cw_ref[...]
    y = cb_ref[...] + sum(
        cw[w:w + 1, :] * xx[TAIL - (MLSTM_CONV - 1) + w:TAIL - (MLSTM_CONV - 1) + w + rows_n, :]
        for w in range(MLSTM_CONV))
    tail_ref[...] = x[rows_n - TAIL:, :]
    qkc_ref[...] = y * _sigmoid(y)

    same, causal = _chunk_masks(SUB)
    tri = jnp.where(causal, 1.0, 0.0).astype(BF16)
    blk = jnp.where(same, 1.0, 0.0).astype(BF16)
    ri = lax.broadcasted_iota(I32, (SUB, SUB), 0)
    ci = lax.broadcasted_iota(I32, (SUB, SUB), 1)
    tri_t = jnp.where(same & (ri <= ci), 1.0, 0.0).astype(BF16)

    nsub = rows_n // SUB
    subs = [slice(i * SUB, (i + 1) * SUB) for i in range(nsub)]
    heads = range(nh)
    chunks = [slice(c * L, (c + 1) * L) for c in range(SUB // L)]
    nck = len(chunks)
    hcol = lambda h: slice(h * dk, (h + 1) * dk)

    ifc = [ifc_ref[rows, :] for rows in subs]
    ifr = [ifr_ref[i] for i in range(nsub)]
    lf_c = [_log_sigmoid(t) for t in ifc]
    cum_c = [_dot_exact_lhs(tri, t) for t in lf_c]
    tot_c = [_dot_exact_lhs(blk, t) for t in lf_c]
    cum_r = [sum(_dot(t, tri_t) for t in _split3(_log_sigmoid(u))) for u in ifr]

    a_end = [[tot_c[i][:, nh + h:nh + h + 1] - cum_c[i][:, nh + h:nh + h + 1] + ifc[i][:, h:h + 1]
              for h in heads] for i in range(nsub)]
    m_prev = [[[None] * nck for _ in heads] for _ in range(nsub)]
    m_next = [[[None] * nck for _ in heads] for _ in range(nsub)]
    for h in heads:
        m_run = m_ref[h:h + 1, 0:1]
        for i in range(nsub):
            for c, cr in enumerate(chunks):
                tot = tot_c[i][c * L:c * L + 1, nh + h:nh + h + 1]
                m_prev[i][h][c] = m_run
                m_run = jnp.maximum(tot + m_run, jnp.max(a_end[i][h][cr], axis=0, keepdims=True))
                m_next[i][h][c] = m_run
        m_ref[h:h + 1, :] = jnp.broadcast_to(m_run, (1, LANES))

    def per_row(vals):
        return jnp.concatenate([jnp.broadcast_to(t, (L, 1)) for t in vals], axis=0)

    qf = [[qkc_ref[rows, hcol(h)] for h in heads] for rows in subs]
    kf = [[qkc_ref[rows, MLSTM_DQK + h * dk:MLSTM_DQK + (h + 1) * dk] * (dk ** -0.5) for h in heads]
          for rows in subs]
    q = [[t.astype(BF16) for t in r] for r in qf]
    k = [[t.astype(BF16) for t in r] for r in kf]
    v = [[v_ref[rows, h * dv:(h + 1) * dv].astype(BF16) for h in heads] for rows in subs]
    qk = [[_dot(q[i][h], k[i][h], NT) for h in heads] for i in range(nsub)]

    kv_t = [[[None] * nck for _ in heads] for _ in range(nsub)]
    n_add = [[[None] * nck for _ in heads] for _ in range(nsub)]
    dec = [[[None] * nck for _ in heads] for _ in range(nsub)]
    for i in range(nsub):
        for h in heads:
            w = jnp.exp(a_end[i][h] - per_row(m_next[i][h]))
            wk = w * kf[i][h]
            for c, cr in enumerate(chunks):
                tot = tot_c[i][c * L:c * L + 1, nh + h:nh + h + 1]
                dec[i][h][c] = jnp.exp(tot + m_prev[i][h][c] - m_next[i][h][c])
                kv_t[i][h][c] = _dot(v[i][h][cr], wk[cr].astype(BF16), TN)
                n_add[i][h][c] = jnp.sum(wk[cr], axis=0, keepdims=True)

    w_inter = [[None] * nh for _ in range(nsub)]
    m_row = [[None] * nh for _ in range(nsub)]
    w_intra = [[None] * nh for _ in range(nsub)]
    for i in range(nsub):
        for h in heads:
            cum_i = cum_c[i][:, nh + h:nh + h + 1]
            log_d = jnp.where(causal, cum_i - cum_r[i][nh + h:nh + h + 1, :] + ifr[i][h:h + 1, :], neg)
            log_inter = cum_i + per_row(m_prev[i][h])
            m_i = jnp.maximum(log_inter, jnp.max(log_d, axis=1, keepdims=True))
            w_intra[i][h] = qk[i][h] * jnp.exp(log_d - m_i)
            w_inter[i][h] = jnp.exp(log_inter - m_i)
            m_row[i][h] = m_i
    num_intra = [[_dot(w_intra[i][h].astype(BF16), v[i][h]) for h in heads] for i in range(nsub)]

    c_prev = [[[None] * nck for _ in heads] for _ in range(nsub)]
    n_prev = [[[None] * nck for _ in heads] for _ in range(nsub)]
    for h in heads:
        ct = ct_ref[h]
        nv = n_ref[h:h + 1, :]
        for i in range(nsub):
            for c in range(nck):
                c_prev[i][h][c] = ct.astype(BF16)
                n_prev[i][h][c] = nv
                ct = dec[i][h][c] * ct + kv_t[i][h][c]
                nv = dec[i][h][c] * nv + n_add[i][h][c]
        ct_ref[h] = ct
        n_ref[h:h + 1, :] = nv

    for i, rows in enumerate(subs):
        outs = []
        for h in heads:
            inter = jnp.concatenate([_dot(q[i][h][cr], c_prev[i][h][c], NT)
                                     for c, cr in enumerate(chunks)], axis=0)
            qn = jnp.concatenate([jnp.sum(qf[i][h][cr] * n_prev[i][h][c], axis=1, keepdims=True)
                                  for c, cr in enumerate(chunks)], axis=0)
            num = w_inter[i][h] * inter + num_intra[i][h]
            den = w_inter[i][h] * qn + jnp.sum(w_intra[i][h], axis=1, keepdims=True)
            outs.append(num / jnp.maximum(jnp.abs(den), jnp.exp(-m_row[i][h])))
        og = og_ref[rows, :]
        o_ref[rows, :] = (jnp.concatenate(outs, axis=1) * _sigmoid(og)).astype(o_ref.dtype)


def mlstm_scan(z, ifr, conv_w, conv_b, b, s):
    t = z.shape[0]
    nc = s // SCAN_ROWS
    cpt = SCAN_ROWS // SUB
    dk = MLSTM_DQK // MLSTM_HEADS
    dv = MLSTM_DV // MLSTM_HEADS
    row = lambda w, col: pl.BlockSpec((SCAN_ROWS, w), lambda bi, c: (bi * nc + c, col))
    return pl.pallas_call(
        _mlstm_kernel,
        grid=(b, nc),
        in_specs=[row(2 * MLSTM_DQK, 0), row(MIX, 2), row(MIX, 3),
                  row(LANES, 4 * MIX // LANES),
                  pl.BlockSpec((cpt, 2 * MLSTM_HEADS, SUB), lambda bi, c: (bi * nc + c, 0, 0)),
                  _resident((MLSTM_CONV, 2 * MLSTM_DQK)), _resident((1, 2 * MLSTM_DQK))],
        out_specs=row(MIX, 0),
        out_shape=jax.ShapeDtypeStruct((t, MIX), BF16),
        scratch_shapes=[pltpu.VMEM((TAIL, 2 * MLSTM_DQK), F32),
                        pltpu.VMEM((MLSTM_HEADS, dv, dk), F32),
                        pltpu.VMEM((8, dk), F32),
                        pltpu.VMEM((8, LANES), F32),
                        pltpu.VMEM((SCAN_ROWS, 2 * MLSTM_DQK), F32)],
        compiler_params=_params("parallel", "arbitrary"),
        name="mlstm_scan",
    )(z, z, z, z, ifr, conv_w, conv_b.reshape(1, -1))


def _merge_kernel(x_ref, g_ref, wg_ref, bg_ref, ya_ref, yb_ref, yc_ref,
                  wa_ref, wb_ref, wc_ref, wo_ref, o_ref):
    x = x_ref[...]
    h = _rms(x, g_ref[...]).astype(BF16)
    d = D_MODEL
    merged = None
    for i, (y_ref, w_ref) in enumerate(((ya_ref, wa_ref), (yb_ref, wb_ref), (yc_ref, wc_ref))):
        gate = _sigmoid(_dot(h, wg_ref[:, i * d:(i + 1) * d]) + bg_ref[:, i * d:(i + 1) * d])
        term = gate * _dot(y_ref[...], w_ref[...])
        merged = term if merged is None else merged + term
    o_ref[...] = x + _dot(merged.astype(BF16), wo_ref[...])


def merge_out(x2d, g, w_gate, b_gate, ya, yb, yc, wa, wb, wc, wo, *, tm):
    t, d = x2d.shape
    row = lambda w: pl.BlockSpec((tm, w), lambda i: (i, 0))
    return pl.pallas_call(
        _merge_kernel,
        grid=(t // tm,),
        in_specs=[row(d), _resident((1, d)), _resident((d, N_BRANCHES * d)),
                  _resident((1, N_BRANCHES * d)), row(MIX), row(MIX), row(MIX),
                  _resident((MIX, d)), _resident((MIX, d)), _resident((MIX, d)),
                  _resident((d, d))],
        out_specs=row(d),
        out_shape=jax.ShapeDtypeStruct((t, d), F32),
        compiler_params=_params("parallel"),
        name="merge_out",
    )(x2d, g.reshape(1, d), w_gate, b_gate.reshape(1, -1), ya, yb, yc, wa, wb, wc, wo)


def _dense_ffn_kernel(x_ref, g_ref, wg_ref, wu_ref, wd_ref, o_ref):
    x = x_ref[...]
    h = _rms(x, g_ref[...]).astype(BF16)
    gt = _dot(h, wg_ref[...])
    up = _dot(h, wu_ref[...])
    act = (gt * _sigmoid(gt) * up).astype(BF16)
    o_ref[...] = x + _dot(act, wd_ref[...])


def dense_ffn(x2d, g, wg, wu, wd, *, tm):
    t, d = x2d.shape
    ff = wg.shape[1]
    return pl.pallas_call(
        _dense_ffn_kernel,
        grid=(t // tm,),
        in_specs=[pl.BlockSpec((tm, d), lambda i: (i, 0)), _resident((1, d)),
                  _resident((d, ff)), _resident((d, ff)), _resident((ff, d))],
        out_specs=pl.BlockSpec((tm, d), lambda i: (i, 0)),
        out_shape=jax.ShapeDtypeStruct((t, d), F32),
        compiler_params=_params("parallel"),
        name="dense_ffn",
    )(x2d, g.reshape(1, d), wg, wu, wd)


ROUTER_TM = 512
MOE_ROWS = 512
META_E, META_G, META_RANK = 0, 2, 4


def _router_kernel(x_ref, g_ref, rw_ref, rb_ref, h_ref, meta_ref, cnt_ref, carry_ref):
    @pl.when(pl.program_id(0) == 0)
    def _():
        carry_ref[...] = jnp.zeros_like(carry_ref)

    h = _rms(x_ref[...], g_ref[...])
    h_ref[...] = h
    tm = h.shape[0]
    lane = lax.broadcasted_iota(I32, (tm, LANES), 1)
    neg = -jnp.inf
    logits = jnp.where(lane < N_EXPERTS, _dot_f32(h, rw_ref[...]) + rb_ref[...], neg)
    m1 = jnp.max(logits, axis=1, keepdims=True)
    e1 = jnp.min(jnp.where(logits == m1, lane, LANES), axis=1, keepdims=True)
    rest = jnp.where(lane == e1, neg, logits)
    m2 = jnp.max(rest, axis=1, keepdims=True)
    e2 = jnp.min(jnp.where(rest == m2, lane, LANES), axis=1, keepdims=True)
    ex = jnp.exp(m2 - m1)
    g1 = 1.0 / (1.0 + ex)
    g2 = ex / (1.0 + ex)
    oh1 = jnp.where(lane == e1, 1.0, 0.0)
    oh2 = jnp.where(lane == e2, 1.0, 0.0)
    cnt = oh1 + oh2
    ri = lax.broadcasted_iota(I32, (tm, tm), 0)
    ci = lax.broadcasted_iota(I32, (tm, tm), 1)
    strict = jnp.where(ci < ri, 1.0, 0.0).astype(BF16)
    excl = _dot(strict, cnt.astype(BF16)) + carry_ref[...]
    r1 = jnp.sum(excl * oh1, axis=1, keepdims=True)
    r2 = jnp.sum(excl * oh2, axis=1, keepdims=True)
    carry_ref[...] += jnp.sum(cnt, axis=0, keepdims=True)
    cnt_ref[...] = carry_ref[...]
    meta = jnp.zeros((tm, LANES), F32)
    for pos, val in ((META_E, e1.astype(F32)), (META_E + 1, e2.astype(F32)),
                     (META_G, g1), (META_G + 1, g2), (META_RANK, r1), (META_RANK + 1, r2)):
        meta = jnp.where(lane == pos, val, meta)
    meta_ref[...] = meta


def moe_router(x2d, g, rw_pad, rb_pad):
    t, d = x2d.shape
    tm = ROUTER_TM
    return pl.pallas_call(
        _router_kernel,
        grid=(t // tm,),
        in_specs=[pl.BlockSpec((tm, d), lambda i: (i, 0)), _resident((1, d)),
                  _resident((d, LANES)), _resident((1, LANES))],
        out_specs=[pl.BlockSpec((tm, d), lambda i: (i, 0)),
                   pl.BlockSpec((tm, LANES), lambda i: (i, 0)),
                   pl.BlockSpec((1, LANES), lambda i: (0, 0))],
        out_shape=[jax.ShapeDtypeStruct((t, d), F32), jax.ShapeDtypeStruct((t, LANES), F32),
                   jax.ShapeDtypeStruct((1, LANES), F32)],
        scratch_shapes=[pltpu.VMEM((1, LANES), F32)],
        compiler_params=_params("arbitrary"),
        name="moe_router",
    )(x2d, g.reshape(1, d), rw_pad, rb_pad)


DISPATCH_TM = 1024
ISSUE_UNROLL = 8


def _row_copy(src_ref, dst_ref, src_row, dst_row, sem):
    return pltpu.make_async_copy(src_ref.at[pl.ds(src_row, 1)], dst_ref.at[pl.ds(dst_row, 1)], sem)


def _dispatch_kernel(d1_ref, d2_ref, h_ref, init_hbm, xs_hbm, sem):
    del init_hbm
    n = h_ref.shape[0]
    per_trip = ISSUE_UNROLL // TOP_K

    def start(i, _):
        for u in range(per_trip):
            r = i * per_trip + u
            _row_copy(h_ref, xs_hbm, r, d1_ref[0, 0, r], sem).start(priority=0)
            _row_copy(h_ref, xs_hbm, r, d2_ref[0, 0, r], sem).start(priority=1)
        return 0

    def wait(i, _):
        for u in range(ISSUE_UNROLL):
            _row_copy(h_ref, xs_hbm, 0, 0, sem).wait()
        return 0

    lax.fori_loop(0, n // per_trip, start, 0)
    lax.fori_loop(0, n // per_trip, wait, 0)


def dispatch_rows(h, dest1, dest2, n_rows):
    t, d = h.shape
    tm = DISPATCH_TM
    nblk = t // tm
    idx = pl.BlockSpec((1, 1, tm), lambda i: (i, 0, 0), memory_space=pltpu.SMEM)
    return pl.pallas_call(
        _dispatch_kernel,
        grid=(nblk,),
        in_specs=[idx, idx, pl.BlockSpec((tm, d), lambda i: (i, 0)),
                  pl.BlockSpec(memory_space=pl.ANY)],
        out_specs=pl.BlockSpec(memory_space=pl.ANY),
        out_shape=jax.ShapeDtypeStruct((n_rows, d), h.dtype),
        input_output_aliases={3: 0},
        scratch_shapes=[pltpu.SemaphoreType.DMA(())],
        compiler_params=_params("arbitrary"),
        name="dispatch_rows",
    )(dest1.reshape(nblk, 1, tm), dest2.reshape(nblk, 1, tm), h, jnp.zeros((n_rows, d), h.dtype))


MXU_DEPTH = 256


def _ff_splits(ff):
    assert ff % MXU_DEPTH == 0
    mid = (ff // MXU_DEPTH + 1) // 2 * MXU_DEPTH
    return ((0, mid), (mid, ff)) if mid < ff else ((0, ff),)


def _grouped_ffn_kernel(be_ref, bv_ref, x_ref, wg_ref, wu_ref, wd_ref, o_ref):
    i = pl.program_id(0)

    @pl.when(bv_ref[i] > 0)
    def _():
        h = x_ref[...].astype(BF16)
        acc = None
        for lo, hi in _ff_splits(wg_ref.shape[1]):
            gt = _dot(h, wg_ref[:, lo:hi])
            up = _dot(h, wu_ref[:, lo:hi])
            act = (gt * _sigmoid(gt) * up).astype(BF16)
            part = _dot(act, wd_ref[lo:hi, :])
            acc = part if acc is None else acc + part
        o_ref[...] = acc

    @pl.when(bv_ref[i] == 0)
    def _():
        o_ref[...] = jnp.zeros_like(o_ref)


def grouped_ffn(xs, blk_e, blk_valid, wg, wu, wd):
    n, d = xs.shape
    ff = wg.shape[2]
    nblk = n // MOE_ROWS
    grid_spec = pltpu.PrefetchScalarGridSpec(
        num_scalar_prefetch=2,
        grid=(nblk,),
        in_specs=[pl.BlockSpec((MOE_ROWS, d), lambda i, be, bv: (i, 0)),
                  pl.BlockSpec((None, d, ff), lambda i, be, bv: (be[i], 0, 0)),
                  pl.BlockSpec((None, d, ff), lambda i, be, bv: (be[i], 0, 0)),
                  pl.BlockSpec((None, ff, d), lambda i, be, bv: (be[i], 0, 0))],
        out_specs=pl.BlockSpec((MOE_ROWS, d), lambda i, be, bv: (i, 0)),
    )
    return pl.pallas_call(
        _grouped_ffn_kernel,
        grid_spec=grid_spec,
        out_shape=jax.ShapeDtypeStruct((n, d), F32),
        compiler_params=_params("arbitrary"),
        name="grouped_ffn",
    )(blk_e, blk_valid, xs, wg, wu, wd)


COMBINE_TM = 1024


def _combine_kernel(d1_ref, d2_ref, x_ref, meta_ref, fg_ref, ys_hbm, o_ref, b1_ref, b2_ref, sem):
    n = x_ref.shape[0]

    def start(i, _):
        for u in range(ISSUE_UNROLL // 2):
            r = i * (ISSUE_UNROLL // 2) + u
            _row_copy(ys_hbm, b1_ref, d1_ref[0, 0, r], r, sem.at[0]).start(priority=0)
            _row_copy(ys_hbm, b2_ref, d2_ref[0, 0, r], r, sem.at[1]).start(priority=1)
        return 0

    def wait(i, _):
        for u in range(ISSUE_UNROLL // 2):
            r = i * (ISSUE_UNROLL // 2) + u
            _row_copy(ys_hbm, b1_ref, 0, r, sem.at[0]).wait()
            _row_copy(ys_hbm, b2_ref, 0, r, sem.at[1]).wait()
        return 0

    lax.fori_loop(0, n // (ISSUE_UNROLL // 2), start, 0)
    lax.fori_loop(0, n // (ISSUE_UNROLL // 2), wait, 0)
    meta = meta_ref[...]
    g1 = meta[:, META_G:META_G + 1]
    g2 = meta[:, META_G + 1:META_G + 2]
    y = x_ref[...] + (g1 * b1_ref[...] + g2 * b2_ref[...])
    o_ref[...] = _rms(y, fg_ref[...])


def moe_combine_norm(x2d, meta, dest1, dest2, ys, final_g):
    t, d = x2d.shape
    tm = COMBINE_TM
    nblk = t // tm
    idx = pl.BlockSpec((1, 1, tm), lambda i: (i, 0, 0), memory_space=pltpu.SMEM)
    return pl.pallas_call(
        _combine_kernel,
        grid=(nblk,),
        in_specs=[idx, idx, pl.BlockSpec((tm, d), lambda i: (i, 0)),
                  pl.BlockSpec((tm, LANES), lambda i: (i, 0)), _resident((1, d)),
                  pl.BlockSpec(memory_space=pl.ANY)],
        out_specs=pl.BlockSpec((tm, d), lambda i: (i, 0)),
        out_shape=jax.ShapeDtypeStruct((t, d), F32),
        scratch_shapes=[pltpu.VMEM((tm, d), F32), pltpu.VMEM((tm, d), F32),
                        pltpu.SemaphoreType.DMA((2,))],
        compiler_params=_params("arbitrary"),
        name="moe_combine_norm",
    )(dest1.reshape(nblk, 1, tm), dest2.reshape(nblk, 1, tm), x2d, meta,
      final_g.reshape(1, d), ys)


def moe_layer_final(x2d, norm_g, router_w, router_b, wg, wu, wd, final_g):
    t, d = x2d.shape
    rw_pad = jnp.pad(router_w, ((0, 0), (0, LANES - N_EXPERTS)))
    rb_pad = jnp.pad(router_b, (0, LANES - N_EXPERTS)).reshape(1, LANES)
    h, meta, counts = moe_router(x2d, norm_g, rw_pad, rb_pad)

    counts = counts[0, :N_EXPERTS].astype(I32)
    padded = (counts + MOE_ROWS - 1) // MOE_ROWS * MOE_ROWS
    pad_ends = jnp.cumsum(padded)
    pad_starts = pad_ends - padded
    e = meta[:, META_E:META_E + TOP_K].astype(I32)
    rank = meta[:, META_RANK:META_RANK + TOP_K].astype(I32)
    dest = pad_starts[e] + rank
    n_blk = (t * TOP_K) // MOE_ROWS + N_EXPERTS
    n_rows = n_blk * MOE_ROWS
    blk_start = jnp.arange(n_blk, dtype=I32) * MOE_ROWS
    blk_valid = (blk_start < pad_ends[-1]).astype(I32)
    blk_e = jnp.minimum(jnp.sum((blk_start[:, None] >= pad_ends[None, :]).astype(I32), axis=1),
                        N_EXPERTS - 1)

    xs = dispatch_rows(h, dest[:, 0], dest[:, 1], n_rows)
    ys = grouped_ffn(xs, blk_e, blk_valid, wg, wu, wd)
    return moe_combine_norm(x2d, meta, dest[:, 0], dest[:, 1], ys, final_g)


def _col_slices():
    sizes = (('moba', 3 * MIX), ('gla_qkv', 2 * GLA_DK + GLA_DV), ('gla_a', GLA_GATE_RANK),
             ('gla_r', GLA_DV), ('mlstm_qkv', 2 * MLSTM_DQK + MLSTM_DV),
             ('mlstm_if', 2 * MLSTM_HEADS), ('mlstm_o', MLSTM_DV),
             ('merge_gate', N_BRANCHES * D_MODEL))
    out, off = {}, 0
    for name, size in sizes:
        out[name] = slice(off, off + size)
        off += size
    return out


def _pad_cols(a, width):
    return jnp.pad(a, ((0, 0), (0, width - a.shape[1])))


def hybrid_mixer_layer(x2d, b, s, norm_g, w_in, b_in, conv_w, conv_b, gate_w, gate_b, gla_norm_g,
                       w_up_moba, w_up_gla, w_up_mlstm, w_o, tables):
    cs = _col_slices()
    bias = b_in.reshape(1, -1)
    wcat = lambda names: jnp.concatenate([w_in[:, cs[n]] for n in names], axis=1)
    bcat = lambda names: jnp.concatenate([bias[:, cs[n]] for n in names], axis=1)

    w_gla = jnp.concatenate([wcat(['gla_qkv', 'gla_r']), _pad_cols(w_in[:, cs['gla_a']], LANES)], axis=1)
    b_gla = jnp.concatenate([bcat(['gla_qkv', 'gla_r']), _pad_cols(bias[:, cs['gla_a']], LANES)], axis=1)
    w_ml = jnp.concatenate([wcat(['mlstm_qkv', 'mlstm_o']), _pad_cols(w_in[:, cs['mlstm_if']], LANES)], axis=1)
    b_ml = jnp.concatenate([bcat(['mlstm_qkv', 'mlstm_o']), _pad_cols(bias[:, cs['mlstm_if']], LANES)], axis=1)
    q, k, vt, km, z_gla, z_ml = mixer_proj(
        x2d, norm_g, w_in[:, cs['moba']].astype(BF16), bias[:, cs['moba']],
        w_gla.astype(BF16), b_gla, w_ml.astype(BF16), b_ml, tables, s, tm=512)

    nb = s // MOBA_BLOCK
    y_moba = moba_attention(q.reshape(b, s, MIX), k.reshape(b, s, MIX),
                            vt.reshape(b, nb, MIX, MOBA_BLOCK),
                            km.reshape(b, nb, MIX)).reshape(b * s, MIX)

    gate_w_pad = jnp.pad(gate_w, ((0, LANES - GLA_GATE_RANK), (0, 0)))
    y_gla = gla_scan(z_gla, gate_w_pad, gate_b, gla_norm_g, b, s)

    n_if = 2 * MLSTM_HEADS
    ifr = z_ml[:, 4 * MIX:4 * MIX + n_if].reshape(b * s // SUB, SUB, n_if).transpose(0, 2, 1)
    y_ml = mlstm_scan(z_ml, ifr, conv_w, conv_b, b, s)

    return merge_out(x2d, norm_g, w_in[:, cs['merge_gate']].astype(BF16), bias[:, cs['merge_gate']],
                     y_moba, y_gla, y_ml, w_up_moba.astype(BF16), w_up_gla.astype(BF16),
                     w_up_mlstm.astype(BF16), w_o.astype(BF16), tm=512)


def kernel(x, attn_norm_g, w_in, b_in, mlstm_conv_w, mlstm_conv_b, gla_gate_w, gla_gate_b, gla_norm_g, w_up_moba, w_up_gla, w_up_mlstm, w_o, ffn_norm_g, ffn_w_gate, ffn_w_up, ffn_w_down, router_w, router_b, moe_w_gate, moe_w_up, moe_w_down, final_norm_g):
    b, s, d = x.shape
    depth = w_in.shape[0]
    assert d == D_MODEL and depth == 2, "two layers: dense SwiGLU then MoE"
    assert s % SCAN_ROWS == 0 and s % MOBA_BLOCK == 0
    x2d = x.reshape(b * s, d)
    tables = _rope_tables(s)
    for l in range(depth):
        x2d = hybrid_mixer_layer(x2d, b, s, attn_norm_g[l], w_in[l], b_in[l], mlstm_conv_w[l],
                                 mlstm_conv_b[l], gla_gate_w[l], gla_gate_b[l], gla_norm_g[l],
                                 w_up_moba[l], w_up_gla[l], w_up_mlstm[l], w_o[l], tables)
        j = l // 2
        if l % 2 == 0:
            x2d = dense_ffn(x2d, ffn_norm_g[l], ffn_w_gate[j].astype(BF16), ffn_w_up[j].astype(BF16),
                            ffn_w_down[j].astype(BF16), tm=512)
        else:
            x2d = moe_layer_final(x2d, ffn_norm_g[l], router_w[j], router_b[j],
                                  moe_w_gate[j].astype(BF16), moe_w_up[j].astype(BF16),
                                  moe_w_down[j].astype(BF16), final_norm_g)
    return x2d.reshape(b, s, d)
```
